```python
import jax, jax.numpy as jnp
from jax import lax
import numpy as np

D_MODEL = 1024
BATCH = 8
SEQ = 2048
DEPTH = 1
DEC_BATCH = 32
DEC_SEQ = 1
PAST_LEN = 8192
PAGE_SIZE = 128

N_META = 16
SB_HEADS = 8
HEAD_DIM = 64
D_ATT = SB_HEADS * HEAD_DIM
D_CONV = D_MODEL // 2
CONV_K = 3
D_FF = -(-8 * D_MODEL // (3 * 256)) * 256
BLOCK = 128
EPS = 1e-6
SB_BIAS_HI = 5.0
SB_BIAS_LO = 9.0
SPLIT_SIZES = (D_ATT, D_ATT, D_ATT, D_CONV, D_CONV, D_CONV, D_MODEL, D_MODEL)
N_IN = sum(SPLIT_SIZES)

kernel_name = 'hybrid_stickbreak_shortconv_step'


def rms_norm(x, g):
    xf = x.astype(jnp.float32)
    y = xf * lax.rsqrt(jnp.mean(xf * xf, axis=-1, keepdims=True) + EPS)
    return (y * g.astype(jnp.float32)).astype(x.dtype)


def split_projection(h, w_in):
    b, t, _ = h.shape
    p = jnp.einsum('btd,dn->btn', h, w_in)
    cuts = [int(c) for c in np.cumsum(SPLIT_SIZES)[:-1]]
    q, k, v, b_gate, c_gate, x_conv, g_att, g_conv = jnp.split(p, cuts, axis=-1)
    heads = lambda a: a.reshape(b, t, SB_HEADS, HEAD_DIM)
    return heads(q), heads(k), heads(v), b_gate, c_gate, x_conv, g_att, g_conv


def stick_breaking(q, k, v, bias, q_pos, k_lo):
    L = k.shape[1]
    z = jnp.einsum('bqhd,bkhd->bhqk', q.astype(jnp.float32), k.astype(jnp.float32)) * (HEAD_DIM ** -0.5)
    z = z + bias.astype(jnp.float32)[None, :, None, None]
    s_idx = jnp.arange(L)
    mask = (s_idx[None, :] < q_pos[:, None]) & (s_idx[None, :] >= k_lo)
    log_beta = jax.nn.log_sigmoid(z)
    log_1m = jnp.where(mask, jax.nn.log_sigmoid(-z), 0.0)
    suffix = lax.cumsum(log_1m, axis=3, reverse=True) - log_1m
    w = jnp.where(mask, jnp.exp(log_beta + suffix), 0.0)
    return jnp.einsum('bhqk,bkhd->bqhd', w, v.astype(jnp.float32))


def prompt_stick_breaking(q, k, v, bias):
    b, t, h, d = q.shape
    pad = (-N_META) % BLOCK
    pw = ((0, 0), (pad, 0), (0, 0), (0, 0))
    qp, kp, vp = jnp.pad(q, pw), jnp.pad(k, pw), jnp.pad(v, pw)
    L = t + pad
    n_blocks = L // BLOCK

    def one_block(i):
        qb = lax.dynamic_slice_in_dim(qp, i * BLOCK, BLOCK, axis=1)
        q_pos = i * BLOCK + jnp.arange(BLOCK)
        return stick_breaking(qb, kp, vp, bias, q_pos, pad)

    o = lax.map(one_block, jnp.arange(n_blocks))
    o = jnp.moveaxis(o, 0, 1).reshape(b, L, h, d)
    return o[:, pad:]


def short_conv(u_ext, w, t):
    return sum(w[i] * u_ext[:, i:i + t] for i in range(CONV_K))


def gated_merge(o_att, y_conv, g_att, g_conv, w_att_out, w_conv_out, w_o):
    b, t, _ = y_conv.shape
    dt = y_conv.dtype
    y_a = jnp.einsum('bte,ed->btd', o_att.reshape(b, t, D_ATT).astype(dt), w_att_out)
    y_b = jnp.einsum('bte,ed->btd', y_conv, w_conv_out)
    m = jax.nn.sigmoid(g_att) * y_a + jax.nn.sigmoid(g_conv) * y_b
    return jnp.einsum('btd,de->bte', m, w_o)


def swiglu(h, w_gate, w_up, w_down):
    a = jax.nn.silu(jnp.einsum('btd,df->btf', h, w_gate)) * jnp.einsum('btd,df->btf', h, w_up)
    return jnp.einsum('btf,fd->btd', a, w_down)


def setup_inputs(seed: int = 0) -> dict:
    key = jax.random.key(seed)
    ks = jax.random.split(key, 20)
    nrm = lambda k, shape, scale: jax.random.normal(k, shape, jnp.float32) * scale
    n_pages = PAST_LEN // PAGE_SIZE
    used = DEC_BATCH * n_pages
    n_phys = used + max(1, used // 4)
    page_table = jax.random.permutation(ks[5], n_phys)[:used].reshape(DEC_BATCH, n_pages).astype(jnp.int32)
    sb_bias = (-jnp.linspace(SB_BIAS_HI, SB_BIAS_LO, SB_HEADS, dtype=jnp.float32)[None, :]
               + nrm(ks[18], (DEPTH, SB_HEADS), 0.1))
    return {
        'x_prompt': nrm(ks[0], (BATCH, SEQ, D_MODEL), 1.0),
        'x_sample': nrm(ks[1], (DEC_BATCH, DEC_SEQ, D_MODEL), 1.0),
        'cache_k': nrm(ks[2], (DEPTH, n_phys, PAGE_SIZE, SB_HEADS, HEAD_DIM), 1.0),
        'cache_v': nrm(ks[3], (DEPTH, n_phys, PAGE_SIZE, SB_HEADS, HEAD_DIM), 1.0),
        'state_conv': nrm(ks[4], (DEPTH, DEC_BATCH, CONV_K - 1, D_CONV), 1.0),
        'page_table': page_table,
        'meta_tokens': nrm(ks[6], (N_META, D_MODEL), 1.0),
        'norm_mix': 1.0 + nrm(ks[7], (DEPTH, D_MODEL), 0.05),
        'w_in': nrm(ks[8], (DEPTH, D_MODEL, N_IN), D_MODEL ** -0.5),
        'sb_bias': sb_bias,
        'conv_w': nrm(ks[9], (DEPTH, CONV_K, D_CONV), CONV_K ** -0.5),
        'w_att_out': nrm(ks[10], (DEPTH, D_ATT, D_MODEL), D_ATT ** -0.5),
        'w_conv_out': nrm(ks[11], (DEPTH, D_CONV, D_MODEL), D_CONV ** -0.5),
        'w_o': nrm(ks[12], (DEPTH, D_MODEL, D_MODEL), D_MODEL ** -0.5),
        'norm_ffn': 1.0 + nrm(ks[13], (DEPTH, D_MODEL), 0.05),
        'w_gate': nrm(ks[14], (DEPTH, D_MODEL, D_FF), D_MODEL ** -0.5),
        'w_up': nrm(ks[15], (DEPTH, D_MODEL, D_FF), D_MODEL ** -0.5),
        'w_down': nrm(ks[16], (DEPTH, D_FF, D_MODEL), D_FF ** -0.5),
        'norm_final': 1.0 + nrm(ks[17], (D_MODEL,), 0.05),
    }


def reference(x_prompt, x_sample, cache_k, cache_v, state_conv, page_table, meta_tokens,
              norm_mix, w_in, sb_bias, conv_w, w_att_out, w_conv_out, w_o, norm_ffn,
              w_gate, w_up, w_down, norm_final):
    n_prompt = x_prompt.shape[0]
    n_dec, t_dec = x_sample.shape[0], x_sample.shape[1]
    meta = jnp.broadcast_to(meta_tokens[None].astype(x_prompt.dtype), (n_prompt, N_META, D_MODEL))
    xp = jnp.concatenate([meta, x_prompt], axis=1)
    xs = x_sample
    kp_l, vp_l, cp_l, ks_l, vs_l, cs_l = [], [], [], [], [], []
    for l in range(DEPTH):
        h = rms_norm(xp, norm_mix[l])
        q, k, v, bg, cg, xc, ga, gb = split_projection(h, w_in[l])
        o_att = prompt_stick_breaking(q, k, v, sb_bias[l])
        u = cg * xc
        u_ext = jnp.concatenate([jnp.zeros((n_prompt, CONV_K - 1, D_CONV), u.dtype), u], axis=1)
        y_conv = bg * short_conv(u_ext, conv_w[l], u.shape[1])
        xp = xp + gated_merge(o_att, y_conv, ga, gb, w_att_out[l], w_conv_out[l], w_o[l])
        xp = xp + swiglu(rms_norm(xp, norm_ffn[l]), w_gate[l], w_up[l], w_down[l])
        kp_l.append(k)
        vp_l.append(v)
        cp_l.append(u_ext[:, -(CONV_K - 1):])
        h = rms_norm(xs, norm_mix[l])
        q, k, v, bg, cg, xc, ga, gb = split_projection(h, w_in[l])
        k_past = cache_k[l][page_table].reshape(n_dec, -1, SB_HEADS, HEAD_DIM)
        v_past = cache_v[l][page_table].reshape(n_dec, -1, SB_HEADS, HEAD_DIM)
        past = k_past.shape[1]
        k_all = jnp.concatenate([k_past.astype(k.dtype), k], axis=1)
        v_all = jnp.concatenate([v_past.astype(v.dtype), v], axis=1)
        o_att = stick_breaking(q, k_all, v_all, sb_bias[l], past + jnp.arange(t_dec), 0)
        u = cg * xc
        u_ext = jnp.concatenate([state_conv[l].astype(u.dtype), u], axis=1)
        y_conv = bg * short_conv(u_ext, conv_w[l], t_dec)
        xs = xs + gated_merge(o_att, y_conv, ga, gb, w_att_out[l], w_conv_out[l], w_o[l])
        xs = xs + swiglu(rms_norm(xs, norm_ffn[l]), w_gate[l], w_up[l], w_down[l])
        ks_l.append(k)
        vs_l.append(v)
        cs_l.append(u_ext[:, -(CONV_K - 1):])
    y_prompt = rms_norm(xp, norm_final)[:, N_META:]
    y_sample = rms_norm(xs, norm_final)
    return (y_prompt, y_sample, jnp.stack(kp_l), jnp.stack(vp_l), jnp.stack(cp_l),
            jnp.stack(ks_l), jnp.stack(vs_l), jnp.stack(cs_l))
```

```python
import functools

import jax
import jax.numpy as jnp
import numpy as np
from jax import lax
from jax.experimental import pallas as pl
from jax.experimental.pallas import tpu as pltpu

F32 = jnp.float32
BF16 = jnp.bfloat16

D_MODEL = 1024
N_META = 16
SB_HEADS = 8
HEAD_DIM = 64
D_ATT = SB_HEADS * HEAD_DIM
D_CONV = 512
CONV_K = 3
EPS = 1e-6
PAGE = 128

LANES = 128
KBLK = 128
N_PAIR = D_ATT // LANES
VMEM_LIMIT = 56 * 1024 * 1024

_NT = (((1,), (1,)), ((), ()))


def _rms(x, g):
    return x * lax.rsqrt(jnp.mean(x * x, axis=-1, keepdims=True) + EPS) * g


def _sigmoid(x):
    return 1.0 / (1.0 + jnp.exp(-x))


def _log_sigmoid_pair(z):
    l = jnp.log(1.0 + jnp.exp(-jnp.abs(z)))
    lb = jnp.minimum(z, 0.0) - l
    return lb, lb - z


def _split_bf16(x):
    hi = x.astype(BF16)
    lo = (x - hi.astype(F32)).astype(BF16)
    return hi, lo


def _const_spec(shape):
    nd = len(shape)
    return pl.BlockSpec(shape, lambda *_: (0,) * nd, pipeline_mode=pl.Buffered(1))


def _proj_kernel(x_ref, g_ref, w_ref, wt_ref, cw_ref, halo_ref,
                 q_ref, ktb_ref, vb_ref, ktf_ref, vtf_ref, yc_ref, st_ref, carry_ref):
    m = pl.program_id(1)
    tm = x_ref.shape[1]
    h = _rms(x_ref[0], g_ref[...]).astype(BF16)
    p = jnp.dot(h, w_ref[...], preferred_element_type=F32)
    pt = lax.dot_general(wt_ref[...], h, _NT, preferred_element_type=F32)
    q_ref[0] = (p[:, 0:D_ATT] * (HEAD_DIM ** -0.5)).astype(BF16)
    vb_ref[0] = p[:, D_ATT:2 * D_ATT].astype(BF16)
    kt = pt[0:D_ATT]
    ktf_ref[0] = kt
    vtf_ref[0] = pt[D_ATT:2 * D_ATT]
    kc = ktb_ref.shape[3]
    for c in range(tm // kc):
        ktb_ref[0, c] = kt[:, c * kc:(c + 1) * kc].astype(BF16)

    bg = p[:, 2 * D_ATT:2 * D_ATT + D_CONV]
    u = p[:, 2 * D_ATT + D_CONV:2 * D_ATT + 2 * D_CONV] * p[:, 2 * D_ATT + 2 * D_CONV:2 * D_ATT + 3 * D_CONV]

    @pl.when(m == 0)
    def _():
        carry_ref[0:2, :] = halo_ref[...]

    prev2 = carry_ref[0:1, :]
    prev1 = carry_ref[1:2, :]
    row = lax.broadcasted_iota(jnp.int32, u.shape, 0)
    u1 = jnp.where(row == 0, prev1, pltpu.roll(u, 1, 0))
    u2 = jnp.where(row == 0, prev2, jnp.where(row == 1, prev1, pltpu.roll(u, 2, 0)))
    cw = cw_ref[...]
    yc = bg * (cw[0:1] * u2 + cw[1:2] * u1 + cw[2:3] * u)
    yc_ref[0] = yc.astype(BF16)
    tail = u[tm - 2:tm, :]
    carry_ref[0:2, :] = tail
    st_ref[0] = tail


def _proj_call(x, g, w, wt, cw, halo, tm):
    b, t, _ = x.shape
    kc = min(KBLK, tm)
    row_spec = lambda width: pl.BlockSpec((1, tm, width), lambda i, j: (i, j, 0))
    col_spec = pl.BlockSpec((1, D_ATT, tm), lambda i, j: (i, 0, j))
    outs = (
        jax.ShapeDtypeStruct((b, t, D_ATT), BF16),
        jax.ShapeDtypeStruct((b, t // kc, D_ATT, kc), BF16),
        jax.ShapeDtypeStruct((b, t, D_ATT), BF16),
        jax.ShapeDtypeStruct((b, D_ATT, t), F32),
        jax.ShapeDtypeStruct((b, D_ATT, t), F32),
        jax.ShapeDtypeStruct((b, t, D_CONV), BF16),
        jax.ShapeDtypeStruct((b, CONV_K - 1, D_CONV), F32),
    )
    return pl.pallas_call(
        _proj_kernel,
        grid=(b, t // tm),
        in_specs=[
            row_spec(D_MODEL),
            _const_spec((1, D_MODEL)),
            _const_spec(w.shape),
            _const_spec(wt.shape),
            _const_spec((CONV_K, D_CONV)),
            _const_spec((CONV_K - 1, D_CONV)),
        ],
        out_specs=(
            row_spec(D_ATT),
            pl.BlockSpec((1, tm // kc, D_ATT, kc), lambda i, j: (i, j, 0, 0)),
            row_spec(D_ATT),
            col_spec, col_spec,
            row_spec(D_CONV),
            pl.BlockSpec((1, CONV_K - 1, D_CONV), lambda i, j: (i, 0, 0)),
        ),
        out_shape=outs,
        scratch_shapes=[pltpu.VMEM((8, D_CONV), F32)],
        compiler_params=pltpu.CompilerParams(
            dimension_semantics=("arbitrary", "arbitrary"), vmem_limit_bytes=VMEM_LIMIT),
        name="proj_conv",
    )(x, g, w, wt, cw, halo)


def _sb_tile(qm, kt, vb, bias, u2, c_ref, mask):
    ws = []
    for hh in range(2):
        z = jnp.dot(qm[hh], kt, preferred_element_type=F32) + bias[hh]
        lb, l1m = _log_sigmoid_pair(z)
        if mask is not None:
            l1m = jnp.where(mask, l1m, 0.0)
        hi, lo = _split_bf16(l1m)
        s_ext = jnp.dot(jnp.concatenate([hi, lo], axis=1), u2, preferred_element_type=F32)
        c = c_ref[hh]
        w = jnp.exp(lb + s_ext[:, :KBLK] + c)
        if mask is not None:
            w = jnp.where(mask, w, 0.0)
        c_ref[hh] = c + s_ext[:, KBLK:]
        ws.append(w.astype(BF16))
    lane = lax.broadcasted_iota(jnp.int32, vb.shape, 1)
    zero = jnp.zeros_like(vb)
    v2 = jnp.concatenate([jnp.where(lane < HEAD_DIM, vb, zero), jnp.where(lane >= HEAD_DIM, vb, zero)], axis=0)
    return jnp.dot(jnp.concatenate(ws, axis=1), v2, preferred_element_type=F32)


def _attn_kernel(bias_ref, q_ref, kt_ref, v_ref, ktm_ref, vm_ref, u2_ref, o_ref, c_ref, acc_ref):
    pair = pl.program_id(1)
    qt = pl.program_id(2)
    q = q_ref[0]
    lane = lax.broadcasted_iota(jnp.int32, q.shape, 1)
    zero = jnp.zeros_like(q)
    qm = (jnp.where(lane < HEAD_DIM, q, zero), jnp.where(lane >= HEAD_DIM, q, zero))
    bias = (bias_ref[2 * pair], bias_ref[2 * pair + 1])
    u2 = u2_ref[...]
    c_ref[...] = jnp.zeros_like(c_ref)

    row = lax.broadcasted_iota(jnp.int32, (KBLK, KBLK), 0)
    col = lax.broadcasted_iota(jnp.int32, (KBLK, KBLK), 1)

    def v_block(j):
        return v_ref[0, pl.ds(pl.multiple_of(j * KBLK, KBLK), KBLK), :]

    acc_ref[...] = _sb_tile(qm, kt_ref[0, qt], v_block(qt), bias, u2, c_ref, col < row)

    def body(i, carry):
        j = qt - 1 - i
        acc_ref[...] += _sb_tile(qm, kt_ref[0, j], v_block(j), bias, u2, c_ref, None)
        return carry

    lax.fori_loop(0, qt, body, 0)

    o = acc_ref[...] + _sb_tile(qm, ktm_ref[...], vm_ref[...], bias, u2, c_ref, col >= KBLK - N_META)
    o_ref[0] = o.astype(o_ref.dtype)


def _attn_call(bias, q, kt, v, ktm, vm, u2):
    b, t, _ = q.shape
    nkb = t // KBLK
    grid_spec = pltpu.PrefetchScalarGridSpec(
        num_scalar_prefetch=1,
        grid=(b, N_PAIR, nkb),
        in_specs=[
            pl.BlockSpec((1, KBLK, LANES), lambda i, p, j, s: (i, j, p)),
            pl.BlockSpec((1, nkb, LANES, KBLK), lambda i, p, j, s: (i, 0, p, 0)),
            pl.BlockSpec((1, t, LANES), lambda i, p, j, s: (i, 0, p)),
            pl.BlockSpec((LANES, KBLK), lambda i, p, j, s: (p, 0)),
            pl.BlockSpec((KBLK, LANES), lambda i, p, j, s: (0, p)),
            pl.BlockSpec((2 * KBLK, 2 * KBLK), lambda i, p, j, s: (0, 0)),
        ],
        out_specs=pl.BlockSpec((1, KBLK, LANES), lambda i, p, j, s: (i, j, p)),
        scratch_shapes=[pltpu.VMEM((2, KBLK, KBLK), F32), pltpu.VMEM((KBLK, LANES), F32)],
    )
    return pl.pallas_call(
        _attn_kernel,
        grid_spec=grid_spec,
        out_shape=jax.ShapeDtypeStruct((b, t, D_ATT), BF16),
        compiler_params=pltpu.CompilerParams(
            dimension_semantics=("arbitrary", "arbitrary", "arbitrary"), vmem_limit_bytes=VMEM_LIMIT),
        name="sb_attn",
    )(bias, q, kt, v, ktm, vm, u2)


def _merge_ffn_kernel(x_ref, o_ref, yc_ref, gmix_ref, wg_ref, wa_ref, wc_ref, wo_ref,
                      gffn_ref, wgate_ref, wup_ref, wdown_ref, gfin_ref, y_ref, *, ff_chunk):
    x = x_ref[...]
    h = _rms(x, gmix_ref[...]).astype(BF16)
    g = jnp.dot(h, wg_ref[...], preferred_element_type=F32)
    ya = jnp.dot(o_ref[...], wa_ref[...], preferred_element_type=F32)
    yb = jnp.dot(yc_ref[...], wc_ref[...], preferred_element_type=F32)
    mrg = _sigmoid(g[:, :D_MODEL]) * ya + _sigmoid(g[:, D_MODEL:]) * yb
    x1 = x + jnp.dot(mrg.astype(BF16), wo_ref[...], preferred_element_type=F32)
    h2 = _rms(x1, gffn_ref[...]).astype(BF16)
    d_ff = wgate_ref.shape[1]
    acc = x1
    for c0 in range(0, d_ff, ff_chunk):
        a = jnp.dot(h2, wgate_ref[:, c0:c0 + ff_chunk], preferred_element_type=F32)
        b = jnp.dot(h2, wup_ref[:, c0:c0 + ff_chunk], preferred_element_type=F32)
        act = (a * _sigmoid(a) * b).astype(BF16)
        acc = acc + jnp.dot(act, wdown_ref[c0:c0 + ff_chunk, :], preferred_element_type=F32)
    y_ref[...] = _rms(acc, gfin_ref[...])


def _merge_ffn_call(x, o, yc, gmix, wg, wa, wc, wo, gffn, wgate, wup, wdown, gfin, tm):
    m = x.shape[0]
    d_ff = wgate.shape[1]
    row_spec = lambda width: pl.BlockSpec((tm, width), lambda i: (i, 0))
    return pl.pallas_call(
        functools.partial(_merge_ffn_kernel, ff_chunk=d_ff // 2),
        grid=(m // tm,),
        in_specs=[
            row_spec(D_MODEL), row_spec(D_ATT), row_spec(D_CONV),
            _const_spec((1, D_MODEL)),
            _const_spec(wg.shape), _const_spec(wa.shape), _const_spec(wc.shape), _const_spec(wo.shape),
            _const_spec((1, D_MODEL)),
            _const_spec(wgate.shape), _const_spec(wup.shape), _const_spec(wdown.shape),
            _const_spec((1, D_MODEL)),
        ],
        out_specs=row_spec(D_MODEL),
        out_shape=jax.ShapeDtypeStruct((m, D_MODEL), F32),
        compiler_params=pltpu.CompilerParams(
            dimension_semantics=("arbitrary",), vmem_limit_bytes=VMEM_LIMIT),
        name="merge_ffn",
    )(x, o, yc, gmix, wg, wa, wc, wo, gffn, wgate, wup, wdown, gfin)


def _sample_proj_kernel(x_ref, g_ref, w_ref, cw_ref, s0_ref, s1_ref, q_ref, kf_ref, vf_ref, yc_ref, u_ref):
    h = _rms(x_ref[...], g_ref[...]).astype(BF16)
    p = jnp.dot(h, w_ref[...], preferred_element_type=F32)
    q_ref[...] = p[:, 0:D_ATT] * (HEAD_DIM ** -0.5)
    kf_ref[...] = p[:, D_ATT:2 * D_ATT]
    vf_ref[...] = p[:, 2 * D_ATT:3 * D_ATT]
    bg = p[:, 3 * D_ATT:3 * D_ATT + D_CONV]
    u = p[:, 3 * D_ATT + D_CONV:3 * D_ATT + 2 * D_CONV] * p[:, 3 * D_ATT + 2 * D_CONV:3 * D_ATT + 3 * D_CONV]
    cw = cw_ref[...]
    yc_ref[...] = (bg * (cw[0:1] * s0_ref[...] + cw[1:2] * s1_ref[...] + cw[2:3] * u)).astype(BF16)
    u_ref[...] = u


def _sample_proj_call(x, g, w, cw, s0, s1):
    n = x.shape[0]
    full = lambda shape: pl.BlockSpec(shape, lambda i: (0,) * len(shape))
    outs = (
        jax.ShapeDtypeStruct((n, D_ATT), F32),
        jax.ShapeDtypeStruct((n, D_ATT), F32),
        jax.ShapeDtypeStruct((n, D_ATT), F32),
        jax.ShapeDtypeStruct((n, D_CONV), BF16),
        jax.ShapeDtypeStruct((n, D_CONV), F32),
    )
    return pl.pallas_call(
        _sample_proj_kernel,
        grid=(1,),
        in_specs=[full(x.shape), full(g.shape), full(w.shape), full(cw.shape), full(s0.shape), full(s1.shape)],
        out_specs=tuple(full(o.shape) for o in outs),
        out_shape=outs,
        compiler_params=pltpu.CompilerParams(vmem_limit_bytes=VMEM_LIMIT),
        name="sample_proj",
    )(x, g, w, cw, s0, s1)


def _decode_kernel(pt_ref, qb_ref, bias_ref, u2_ref, ones_ref, *refs, pages_per_step):
    k_refs = refs[:pages_per_step]
    v_refs = refs[pages_per_step:2 * pages_per_step]
    o_ref = refs[2 * pages_per_step]
    c_ref, acc_ref = refs[2 * pages_per_step + 1:]
    step = pl.program_id(1)

    @pl.when(step == 0)
    def _():
        c_ref[...] = jnp.zeros_like(c_ref)
        acc_ref[...] = jnp.zeros_like(acc_ref)

    qb = qb_ref[0]
    zs = []
    for i in range(pages_per_step):
        prod = k_refs[i][0] * qb
        zs.append(prod.reshape(SB_HEADS, HEAD_DIM, PAGE).sum(axis=1) + bias_ref[...])
    z = jnp.concatenate(zs, axis=0)
    lb, l1m = _log_sigmoid_pair(z)
    hi, lo = _split_bf16(l1m)
    s_ext = jnp.dot(jnp.concatenate([hi, lo], axis=1), u2_ref[...], preferred_element_type=F32)
    c = c_ref[...]
    contrib = None
    for i in range(pages_per_step):
        rows = slice(i * SB_HEADS, (i + 1) * SB_HEADS)
        w = jnp.exp(lb[rows] + s_ext[rows, :PAGE] + c)
        c = c + s_ext[rows, PAGE:]
        wexp = jnp.broadcast_to(w[:, None, :], (SB_HEADS, HEAD_DIM, PAGE)).reshape(D_ATT, PAGE)
        term = v_refs[i][0] * wexp
        contrib = term if contrib is None else contrib + term
    c_ref[...] = c
    acc_ref[...] += contrib

    @pl.when(step == pl.num_programs(1) - 1)
    def _():
        acc = acc_ref[...]
        a0 = acc.astype(BF16)
        r1 = acc - a0.astype(F32)
        a1 = r1.astype(BF16)
        a2 = (r1 - a1.astype(F32)).astype(BF16)
        ones = ones_ref[...]
        tot = (lax.dot_general(ones, a0, _NT, preferred_element_type=F32)
               + lax.dot_general(ones, a1, _NT, preferred_element_type=F32)
               + lax.dot_general(ones, a2, _NT, preferred_element_type=F32))
        o_ref[0] = tot[0:1].astype(o_ref.dtype)


def _decode_call(page_table, qb, bias_b, u2, ones, cache_kt, cache_vt, pages_per_step):
    n_dec, n_pages = page_table.shape
    steps = n_pages // pages_per_step

    def page_spec(i):
        def index_map(b, c, pt):
            return (pt[b * n_pages + (n_pages - 1 - (c * pages_per_step + i))], 0, 0)
        return pl.BlockSpec((1, D_ATT, PAGE), index_map)

    grid_spec = pltpu.PrefetchScalarGridSpec(
        num_scalar_prefetch=1,
        grid=(n_dec, steps),
        in_specs=[
            pl.BlockSpec((1, D_ATT, LANES), lambda b, c, pt: (b, 0, 0)),
            pl.BlockSpec((SB_HEADS, LANES), lambda b, c, pt: (0, 0)),
            pl.BlockSpec((2 * PAGE, 2 * PAGE), lambda b, c, pt: (0, 0)),
            pl.BlockSpec((8, LANES), lambda b, c, pt: (0, 0)),
        ] + [page_spec(i) for i in range(pages_per_step)] * 2,
        out_specs=pl.BlockSpec((1, 1, D_ATT), lambda b, c, pt: (b, 0, 0)),
        scratch_shapes=[pltpu.VMEM((SB_HEADS, LANES), F32), pltpu.VMEM((D_ATT, PAGE), F32)],
    )
    return pl.pallas_call(
        functools.partial(_decode_kernel, pages_per_step=pages_per_step),
        grid_spec=grid_spec,
        out_shape=jax.ShapeDtypeStruct((n_dec, 1, D_ATT), BF16),
        compiler_params=pltpu.CompilerParams(
            dimension_semantics=("arbitrary", "arbitrary"), vmem_limit_bytes=VMEM_LIMIT),
        name="sb_decode",
    )(page_table.reshape(-1), qb, bias_b, u2, ones,
      *([cache_kt] * pages_per_step), *([cache_vt] * pages_per_step))


def _suffix_cols():
    j = np.arange(KBLK)[:, None]
    s = np.arange(KBLK)[None, :]
    blk = np.concatenate([(j > s).astype(np.float32), np.ones((KBLK, KBLK), np.float32)], axis=1)
    return jnp.asarray(np.concatenate([blk, blk], axis=0), dtype=BF16)


def _pages_as_rows(cache):
    n_phys = cache.shape[0]
    return jnp.transpose(cache, (0, 2, 3, 1)).reshape(n_phys, D_ATT, PAGE)


def _state_from_rows(meta_t, real_t):
    b = real_t.shape[0]
    full = jnp.concatenate([jnp.broadcast_to(meta_t, (b,) + meta_t.shape[1:]), real_t], axis=2)
    full = full.reshape(b, SB_HEADS, HEAD_DIM, full.shape[2])
    return jnp.transpose(full, (0, 3, 1, 2))[None]


def kernel(x_prompt, x_sample, cache_k, cache_v, state_conv, page_table, meta_tokens, norm_mix, w_in,
           sb_bias, conv_w, w_att_out, w_conv_out, w_o, norm_ffn, w_gate, w_up, w_down, norm_final):
    assert w_in.shape[0] == 1, "single-layer step: the meta rows' block output never reaches a returned array"
    n_prompt, seq, _ = x_prompt.shape
    n_dec, t_dec, _ = x_sample.shape
    assert t_dec == 1

    w_in_b = w_in[0].astype(BF16)
    w_q, w_k, w_v = (w_in_b[:, i * D_ATT:(i + 1) * D_ATT] for i in range(3))
    n_qkvc = 3 * D_ATT + 3 * D_CONV
    w_conv3 = w_in_b[:, 3 * D_ATT:n_qkvc]
    w_gates = w_in_b[:, n_qkvc:]
    w_rows = jnp.concatenate([w_q, w_v, w_conv3], axis=1)
    w_cols = jnp.concatenate([w_k, w_v], axis=1).T
    gmix = norm_mix[0][None, :]
    gffn = norm_ffn[0][None, :]
    gfin = norm_final[None, :]
    cw = conv_w[0]
    bias = sb_bias[0]
    ffn_w = (w_att_out[0].astype(BF16), w_conv_out[0].astype(BF16), w_o[0].astype(BF16), gffn,
             w_gate[0].astype(BF16), w_up[0].astype(BF16), w_down[0].astype(BF16), gfin)
    u2 = _suffix_cols()

    zero_halo = jnp.zeros((CONV_K - 1, D_CONV), F32)
    _, ktm_b, vm_b, ktm_f, vtm_f, _, u_meta = _proj_call(
        meta_tokens[None], gmix, w_rows, w_cols, cw, zero_halo, N_META)

    q_b, kt_b, v_b, kt_f, vt_f, yc, conv_p = _proj_call(x_prompt, gmix, w_rows, w_cols, cw, u_meta[0], 512)
    ktm_pad = jnp.pad(ktm_b[0, 0], ((0, 0), (KBLK - N_META, 0)))
    vm_pad = jnp.pad(vm_b[0], ((KBLK - N_META, 0), (0, 0)))
    o_att = _attn_call(bias, q_b, kt_b, v_b, ktm_pad, vm_pad, u2)
    m_rows = n_prompt * seq
    y_p = _merge_ffn_call(x_prompt.reshape(m_rows, D_MODEL), o_att.reshape(m_rows, D_ATT),
                          yc.reshape(m_rows, D_CONV), gmix, w_gates, *ffn_w, tm=512)

    st = state_conv[0]
    q_s, k_s, v_s, yc_s, u_s = _sample_proj_call(x_sample[:, 0], gmix, w_in_b[:, :n_qkvc], cw, st[:, 0], st[:, 1])
    qb = jnp.broadcast_to(q_s[:, :, None], (n_dec, D_ATT, LANES))
    bias_b = jnp.broadcast_to(bias[:, None], (SB_HEADS, LANES))
    o_s = _decode_call(page_table, qb, bias_b, u2, jnp.ones((8, LANES), BF16),
                       _pages_as_rows(cache_k[0]), _pages_as_rows(cache_v[0]), 8)
    y_s = _merge_ffn_call(x_sample[:, 0], o_s[:, 0], yc_s, gmix, w_gates, *ffn_w, tm=n_dec)

    return (
        y_p.reshape(n_prompt, seq, D_MODEL),
        y_s.reshape(n_dec, 1, D_MODEL),
        _state_from_rows(ktm_f, kt_f),
        _state_from_rows(vtm_f, vt_f),
        conv_p[None],
        k_s.reshape(1, n_dec, 1, SB_HEADS, HEAD_DIM),
        v_s.reshape(1, n_dec, 1, SB_HEADS, HEAD_DIM),
        jnp.stack([st[:, 1], u_s], axis=1)[None],
    )
```

```python
import functools

import jax
import jax.numpy as jnp
import numpy as np
from jax import lax
from jax.experimental import pallas as pl
from jax.experimental.pallas import tpu as pltpu

F32 = jnp.float32
BF16 = jnp.bfloat16

D_MODEL = 1024
N_META = 16
SB_HEADS = 8
HEAD_DIM = 64
D_ATT = SB_HEADS * HEAD_DIM
D_CONV = 512
CONV_K = 3
EPS = 1e-6
PAGE = 128

LANES = 128
KBLK = 128
N_PAIR = D_ATT // LANES
VMEM_LIMIT = 56 * 1024 * 1024
LOG2E = 1.4426950408889634

_NT = (((1,), (1,)), ((), ()))


def _rms(x, g):
    return x * lax.rsqrt(jnp.mean(x * x, axis=-1, keepdims=True) + EPS) * g


def _sigmoid(x):
    return 1.0 / (1.0 + jnp.exp(-x))


def _log_sigmoid_pair(z):
    l = jnp.log(1.0 + jnp.exp(-jnp.abs(z)))
    lb = jnp.minimum(z, 0.0) - l
    return lb, lb - z


def _split_bf16(x):
    hi = x.astype(BF16)
    lo = (x - hi.astype(F32)).astype(BF16)
    return hi, lo


def _const_spec(shape):
    nd = len(shape)
    return pl.BlockSpec(shape, lambda *_: (0,) * nd, pipeline_mode=pl.Buffered(1))


def _proj_kernel(x_ref, g_ref, w_ref, wt_ref, cw_ref, halo_ref,
                 q_ref, ktb_ref, vb_ref, ktf_ref, vtf_ref, yc_ref, st_ref, carry_ref):
    m = pl.program_id(1)
    tm = x_ref.shape[1]
    h = _rms(x_ref[0], g_ref[...]).astype(BF16)
    p = jnp.dot(h, w_ref[...], preferred_element_type=F32)
    pt = lax.dot_general(wt_ref[...], h, _NT, preferred_element_type=F32)
    q_ref[0] = (p[:, 0:D_ATT] * (HEAD_DIM ** -0.5 * LOG2E)).astype(BF16)
    kt = pt[0:D_ATT]
    ktf_ref[0] = kt
    vtf_ref[0] = pt[D_ATT:2 * D_ATT]
    ktb = kt.astype(BF16)
    vb = p[:, D_ATT:2 * D_ATT].astype(BF16)
    kc = vb_ref.shape[3] // 2
    zk = jnp.zeros((HEAD_DIM, kc), BF16)
    lane = lax.broadcasted_iota(jnp.int32, (kc, LANES), 1)
    zv = jnp.zeros((kc, LANES), BF16)
    for c in range(tm // kc):
        for pr in range(N_PAIR):
            kblk = ktb[pr * LANES:(pr + 1) * LANES, c * kc:(c + 1) * kc]
            ktb_ref[0, c, pr] = jnp.concatenate(
                [jnp.concatenate([kblk[:HEAD_DIM], zk], axis=0),
                 jnp.concatenate([zk, kblk[HEAD_DIM:]], axis=0)], axis=1)
            vblk = vb[c * kc:(c + 1) * kc, pr * LANES:(pr + 1) * LANES]
            vb_ref[0, c, pr] = jnp.concatenate(
                [jnp.where(lane < HEAD_DIM, vblk, zv), jnp.where(lane >= HEAD_DIM, vblk, zv)], axis=0)

    bg = p[:, 2 * D_ATT:2 * D_ATT + D_CONV]
    u = p[:, 2 * D_ATT + D_CONV:2 * D_ATT + 2 * D_CONV] * p[:, 2 * D_ATT + 2 * D_CONV:2 * D_ATT + 3 * D_CONV]

    @pl.when(m == 0)
    def _():
        carry_ref[0:2, :] = halo_ref[...]

    prev2 = carry_ref[0:1, :]
    prev1 = carry_ref[1:2, :]
    row = lax.broadcasted_iota(jnp.int32, u.shape, 0)
    u1 = jnp.where(row == 0, prev1, pltpu.roll(u, 1, 0))
    u2 = jnp.where(row == 0, prev2, jnp.where(row == 1, prev1, pltpu.roll(u, 2, 0)))
    cw = cw_ref[...]
    yc = bg * (cw[0:1] * u2 + cw[1:2] * u1 + cw[2:3] * u)
    yc_ref[0] = yc.astype(BF16)
    tail = u[tm - 2:tm, :]
    carry_ref[0:2, :] = tail
    st_ref[0] = tail


def _proj_call(x, g, w, wt, cw, halo, tm):
    b, t, _ = x.shape
    kc = min(KBLK, tm)
    row_spec = lambda width: pl.BlockSpec((1, tm, width), lambda i, j: (i, j, 0))
    col_spec = pl.BlockSpec((1, D_ATT, tm), lambda i, j: (i, 0, j))
    outs = (
        jax.ShapeDtypeStruct((b, t, D_ATT), BF16),
        jax.ShapeDtypeStruct((b, t // kc, N_PAIR, LANES, 2 * kc), BF16),
        jax.ShapeDtypeStruct((b, t // kc, N_PAIR, 2 * kc, LANES), BF16),
        jax.ShapeDtypeStruct((b, D_ATT, t), F32),
        jax.ShapeDtypeStruct((b, D_ATT, t), F32),
        jax.ShapeDtypeStruct((b, t, D_CONV), BF16),
        jax.ShapeDtypeStruct((b, CONV_K - 1, D_CONV), F32),
    )
    return pl.pallas_call(
        _proj_kernel,
        grid=(b, t // tm),
        in_specs=[
            row_spec(D_MODEL),
            _const_spec((1, D_MODEL)),
            _const_spec(w.shape),
            _const_spec(wt.shape),
            _const_spec((CONV_K, D_CONV)),
            _const_spec((CONV_K - 1, D_CONV)),
        ],
        out_specs=(
            row_spec(D_ATT),
            pl.BlockSpec((1, tm // kc, N_PAIR, LANES, 2 * kc), lambda i, j: (i, j, 0, 0, 0)),
            pl.BlockSpec((1, tm // kc, N_PAIR, 2 * kc, LANES), lambda i, j: (i, j, 0, 0, 0)),
            col_spec, col_spec,
            row_spec(D_CONV),
            pl.BlockSpec((1, CONV_K - 1, D_CONV), lambda i, j: (i, 0, 0)),
        ),
        out_shape=outs,
        scratch_shapes=[pltpu.VMEM((8, D_CONV), F32)],
        compiler_params=pltpu.CompilerParams(
            dimension_semantics=("arbitrary", "arbitrary"), vmem_limit_bytes=VMEM_LIMIT),
        name="proj_conv",
    )(x, g, w, wt, cw, halo)


def _sb_weights(z2, bias, u2, c_ref, r0, mask):
    ws = []
    for hh in range(2):
        z = z2[:, hh * KBLK:(hh + 1) * KBLK] + bias[hh]
        l = jnp.log(1.0 + jnp.exp2(-jnp.abs(z))) * LOG2E
        lb = jnp.minimum(z, 0.0) - l
        l1m = lb - z
        if mask is not None:
            l1m = jnp.where(mask, l1m, 0.0)
        hi, lo = _split_bf16(l1m)
        s_ext = jnp.dot(jnp.concatenate([hi, lo], axis=1), u2, preferred_element_type=F32)
        c = c_ref[hh, r0:, :]
        w = jnp.exp2(lb + s_ext[:, :KBLK] + c)
        if mask is not None:
            w = jnp.where(mask, w, 0.0)
        c_ref[hh, r0:, :] = c + s_ext[:, KBLK:]
        ws.append(w.astype(BF16))
    return jnp.concatenate(ws, axis=1)


def _attn_kernel(bias_ref, q_ref, kt_ref, v_ref, ktm_ref, vm_ref, u2_ref, o_ref, c_ref, acc_ref):
    pair = pl.program_id(1)
    qt = pl.program_id(2)
    tq = q_ref.shape[0]
    nsub = tq // KBLK
    bias = (bias_ref[2 * pair], bias_ref[2 * pair + 1])
    u2 = u2_ref[...]
    c_ref[...] = jnp.zeros_like(c_ref)
    acc_ref[...] = jnp.zeros_like(acc_ref)

    def scores(q, k2):
        return jnp.dot(q, k2, preferred_element_type=F32)

    def add_pv(w2, v2, r0):
        acc_ref[r0:, :] += jnp.dot(w2, v2, preferred_element_type=F32)

    for jj in reversed(range(nsub)):
        r0 = jj * KBLK
        row = lax.broadcasted_iota(jnp.int32, (tq - r0, KBLK), 0)
        col = lax.broadcasted_iota(jnp.int32, (tq - r0, KBLK), 1)
        j = qt * nsub + jj
        add_pv(_sb_weights(scores(q_ref[r0:, :], kt_ref[j]), bias, u2, c_ref, r0, col < row), v_ref[j], r0)

    def body(i, carry):
        for jj in range(nsub):
            j = (qt - i) * nsub - 1 - jj
            add_pv(_sb_weights(scores(q_ref[...], kt_ref[j]), bias, u2, c_ref, 0, None), v_ref[j], 0)
        return carry

    lax.fori_loop(0, qt, body, 0)

    col = lax.broadcasted_iota(jnp.int32, (tq, KBLK), 1)
    add_pv(_sb_weights(scores(q_ref[...], ktm_ref[...]), bias, u2, c_ref, 0, col >= KBLK - N_META),
           vm_ref[...], 0)
    o_ref[...] = acc_ref[...].astype(o_ref.dtype)


def _attn_call(bias, q, k2, v2, k2m, v2m, u2, tq):
    b, t, _ = q.shape
    nkb = t // KBLK
    grid_spec = pltpu.PrefetchScalarGridSpec(
        num_scalar_prefetch=1,
        grid=(b, N_PAIR, t // tq),
        in_specs=[
            pl.BlockSpec((None, tq, LANES), lambda i, p, j, s: (i, j, p)),
            pl.BlockSpec((None, nkb, None, LANES, 2 * KBLK), lambda i, p, j, s: (i, 0, p, 0, 0)),
            pl.BlockSpec((None, nkb, None, 2 * KBLK, LANES), lambda i, p, j, s: (i, 0, p, 0, 0)),
            pl.BlockSpec((None, LANES, 2 * KBLK), lambda i, p, j, s: (p, 0, 0)),
            pl.BlockSpec((None, 2 * KBLK, LANES), lambda i, p, j, s: (p, 0, 0)),
            pl.BlockSpec((2 * KBLK, 2 * KBLK), lambda i, p, j, s: (0, 0)),
        ],
        out_specs=pl.BlockSpec((None, tq, LANES), lambda i, p, j, s: (i, j, p)),
        scratch_shapes=[pltpu.VMEM((2, tq, KBLK), F32), pltpu.VMEM((tq, LANES), F32)],
    )
    return pl.pallas_call(
        _attn_kernel,
        grid_spec=grid_spec,
        out_shape=jax.ShapeDtypeStruct((b, t, D_ATT), BF16),
        compiler_params=pltpu.CompilerParams(
            dimension_semantics=("arbitrary", "arbitrary", "arbitrary"), vmem_limit_bytes=VMEM_LIMIT),
        name="sb_attn",
    )(bias, q, k2, v2, k2m, v2m, u2)


def _merge_ffn_kernel(x_ref, o_ref, yc_ref, gmix_ref, wg_ref, wa_ref, wc_ref, wo_ref,
                      gffn_ref, wgate_ref, wup_ref, wdown_ref, gfin_ref, y_ref, *, ff_chunk):
    x = x_ref[...]
    h = _rms(x, gmix_ref[...]).astype(BF16)
    g = jnp.dot(h, wg_ref[...], preferred_element_type=F32)
    ya = jnp.dot(o_ref[...], wa_ref[...], preferred_element_type=F32)
    yb = jnp.dot(yc_ref[...], wc_ref[...], preferred_element_type=F32)
    mrg = _sigmoid(g[:, :D_MODEL]) * ya + _sigmoid(g[:, D_MODEL:]) * yb
    x1 = x + jnp.dot(mrg.astype(BF16), wo_ref[...], preferred_element_type=F32)
    h2 = _rms(x1, gffn_ref[...]).astype(BF16)
    d_ff = wgate_ref.shape[1]
    acc = x1
    for c0 in range(0, d_ff, ff_chunk):
        a = jnp.dot(h2, wgate_ref[:, c0:c0 + ff_chunk], preferred_element_type=F32)
        b = jnp.dot(h2, wup_ref[:, c0:c0 + ff_chunk], preferred_element_type=F32)
        act = (a * _sigmoid(a) * b).astype(BF16)
        acc = acc + jnp.dot(act, wdown_ref[c0:c0 + ff_chunk, :], preferred_element_type=F32)
    y_ref[...] = _rms(acc, gfin_ref[...])


def _merge_ffn_call(x, o, yc, gmix, wg, wa, wc, wo, gffn, wgate, wup, wdown, gfin, tm):
    m = x.shape[0]
    d_ff = wgate.shape[1]
    row_spec = lambda width: pl.BlockSpec((tm, width), lambda i: (i, 0))
    return pl.pallas_call(
        functools.partial(_merge_ffn_kernel, ff_chunk=d_ff // 2),
        grid=(m // tm,),
        in_specs=[
            row_spec(D_MODEL), row_spec(D_ATT), row_spec(D_CONV),
            _const_spec((1, D_MODEL)),
            _const_spec(wg.shape), _const_spec(wa.shape), _const_spec(wc.shape), _const_spec(wo.shape),
            _const_spec((1, D_MODEL)),
            _const_spec(wgate.shape), _const_spec(wup.shape), _const_spec(wdown.shape),
            _const_spec((1, D_MODEL)),
        ],
        out_specs=row_spec(D_MODEL),
        out_shape=jax.ShapeDtypeStruct((m, D_MODEL), F32),
        compiler_params=pltpu.CompilerParams(
            dimension_semantics=("arbitrary",), vmem_limit_bytes=VMEM_LIMIT),
        name="merge_ffn",
    )(x, o, yc, gmix, wg, wa, wc, wo, gffn, wgate, wup, wdown, gfin)


def _sample_proj_kernel(x_ref, g_ref, w_ref, cw_ref, s0_ref, s1_ref, q_ref, kf_ref, vf_ref, yc_ref, u_ref):
    h = _rms(x_ref[...], g_ref[...]).astype(BF16)
    p = jnp.dot(h, w_ref[...], preferred_element_type=F32)
    q_ref[...] = p[:, 0:D_ATT] * (HEAD_DIM ** -0.5)
    kf_ref[...] = p[:, D_ATT:2 * D_ATT]
    vf_ref[...] = p[:, 2 * D_ATT:3 * D_ATT]
    bg = p[:, 3 * D_ATT:3 * D_ATT + D_CONV]
    u = p[:, 3 * D_ATT + D_CONV:3 * D_ATT + 2 * D_CONV] * p[:, 3 * D_ATT + 2 * D_CONV:3 * D_ATT + 3 * D_CONV]
    cw = cw_ref[...]
    yc_ref[...] = (bg * (cw[0:1] * s0_ref[...] + cw[1:2] * s1_ref[...] + cw[2:3] * u)).astype(BF16)
    u_ref[...] = u


def _sample_proj_call(x, g, w, cw, s0, s1):
    n = x.shape[0]
    full = lambda shape: pl.BlockSpec(shape, lambda i: (0,) * len(shape))
    outs = (
        jax.ShapeDtypeStruct((n, D_ATT), F32),
        jax.ShapeDtypeStruct((n, D_ATT), F32),
        jax.ShapeDtypeStruct((n, D_ATT), F32),
        jax.ShapeDtypeStruct((n, D_CONV), BF16),
        jax.ShapeDtypeStruct((n, D_CONV), F32),
    )
    return pl.pallas_call(
        _sample_proj_kernel,
        grid=(1,),
        in_specs=[full(x.shape), full(g.shape), full(w.shape), full(cw.shape), full(s0.shape), full(s1.shape)],
        out_specs=tuple(full(o.shape) for o in outs),
        out_shape=outs,
        compiler_params=pltpu.CompilerParams(vmem_limit_bytes=VMEM_LIMIT),
        name="sample_proj",
    )(x, g, w, cw, s0, s1)


def _decode_kernel(pt_ref, qb_ref, bias_ref, u2_ref, ones_ref, *refs, pages_per_step):
    k_refs = refs[:pages_per_step]
    v_refs = refs[pages_per_step:2 * pages_per_step]
    o_ref = refs[2 * pages_per_step]
    c_ref, acc_ref = refs[2 * pages_per_step + 1:]
    step = pl.program_id(1)

    @pl.when(step == 0)
    def _():
        c_ref[...] = jnp.zeros_like(c_ref)
        acc_ref[...] = jnp.zeros_like(acc_ref)

    qb = qb_ref[0]
    zs = []
    for i in range(pages_per_step):
        prod = k_refs[i][0] * qb
        zs.append(prod.reshape(SB_HEADS, HEAD_DIM, PAGE).sum(axis=1) + bias_ref[...])
    z = jnp.concatenate(zs, axis=0)
    lb, l1m = _log_sigmoid_pair(z)
    hi, lo = _split_bf16(l1m)
    s_ext = jnp.dot(jnp.concatenate([hi, lo], axis=1), u2_ref[...], preferred_element_type=F32)
    c = c_ref[...]
    contrib = None
    for i in range(pages_per_step):
        rows = slice(i * SB_HEADS, (i + 1) * SB_HEADS)
        w = jnp.exp(lb[rows] + s_ext[rows, :PAGE] + c)
        c = c + s_ext[rows, PAGE:]
        wexp = jnp.broadcast_to(w[:, None, :], (SB_HEADS, HEAD_DIM, PAGE)).reshape(D_ATT, PAGE)
        term = v_refs[i][0] * wexp
        contrib = term if contrib is None else contrib + term
    c_ref[...] = c
    acc_ref[...] += contrib

    @pl.when(step == pl.num_programs(1) - 1)
    def _():
        acc = acc_ref[...]
        a0 = acc.astype(BF16)
        r1 = acc - a0.astype(F32)
        a1 = r1.astype(BF16)
        a2 = (r1 - a1.astype(F32)).astype(BF16)
        ones = ones_ref[...]
        tot = (lax.dot_general(ones, a0, _NT, preferred_element_type=F32)
               + lax.dot_general(ones, a1, _NT, preferred_element_type=F32)
               + lax.dot_general(ones, a2, _NT, preferred_element_type=F32))
        o_ref[0] = tot[0:1].astype(o_ref.dtype)


def _decode_call(page_table, qb, bias_b, u2, ones, cache_kt, cache_vt, pages_per_step):
    n_dec, n_pages = page_table.shape
    steps = n_pages // pages_per_step

    def page_spec(i):
        def index_map(b, c, pt):
            return (pt[b * n_pages + (n_pages - 1 - (c * pages_per_step + i))], 0, 0)
        return pl.BlockSpec((1, D_ATT, PAGE), index_map)

    grid_spec = pltpu.PrefetchScalarGridSpec(
        num_scalar_prefetch=1,
        grid=(n_dec, steps),
        in_specs=[
            pl.BlockSpec((1, D_ATT, LANES), lambda b, c, pt: (b, 0, 0)),
            pl.BlockSpec((SB_HEADS, LANES), lambda b, c, pt: (0, 0)),
            pl.BlockSpec((2 * PAGE, 2 * PAGE), lambda b, c, pt: (0, 0)),
            pl.BlockSpec((8, LANES), lambda b, c, pt: (0, 0)),
        ] + [page_spec(i) for i in range(pages_per_step)] * 2,
        out_specs=pl.BlockSpec((1, 1, D_ATT), lambda b, c, pt: (b, 0, 0)),
        scratch_shapes=[pltpu.VMEM((SB_HEADS, LANES), F32), pltpu.VMEM((D_ATT, PAGE), F32)],
    )
    return pl.pallas_call(
        functools.partial(_decode_kernel, pages_per_step=pages_per_step),
        grid_spec=grid_spec,
        out_shape=jax.ShapeDtypeStruct((n_dec, 1, D_ATT), BF16),
        compiler_params=pltpu.CompilerParams(
            dimension_semantics=("arbitrary", "arbitrary"), vmem_limit_bytes=VMEM_LIMIT),
        name="sb_decode",
    )(page_table.reshape(-1), qb, bias_b, u2, ones,
      *([cache_kt] * pages_per_step), *([cache_vt] * pages_per_step))


def _suffix_cols():
    j = np.arange(KBLK)[:, None]
    s = np.arange(KBLK)[None, :]
    blk = np.concatenate([(j > s).astype(np.float32), np.ones((KBLK, KBLK), np.float32)], axis=1)
    return jnp.asarray(np.concatenate([blk, blk], axis=0), dtype=BF16)


def _pages_as_rows(cache):
    n_phys = cache.shape[0]
    return jnp.transpose(cache, (0, 2, 3, 1)).reshape(n_phys, D_ATT, PAGE)


def _state_from_rows(meta_t, real_t):
    b = real_t.shape[0]
    full = jnp.concatenate([jnp.broadcast_to(meta_t, (b,) + meta_t.shape[1:]), real_t], axis=2)
    full = full.reshape(b, SB_HEADS, HEAD_DIM, full.shape[2])
    return jnp.transpose(full, (0, 3, 1, 2))[None]


def kernel(x_prompt, x_sample, cache_k, cache_v, state_conv, page_table, meta_tokens, norm_mix, w_in,
           sb_bias, conv_w, w_att_out, w_conv_out, w_o, norm_ffn, w_gate, w_up, w_down, norm_final):
    assert w_in.shape[0] == 1, "single-layer step: the meta rows' block output never reaches a returned array"
    n_prompt, seq, _ = x_prompt.shape
    n_dec, t_dec, _ = x_sample.shape
    assert t_dec == 1

    w_in_b = w_in[0].astype(BF16)
    w_q, w_k, w_v = (w_in_b[:, i * D_ATT:(i + 1) * D_ATT] for i in range(3))
    n_qkvc = 3 * D_ATT + 3 * D_CONV
    w_conv3 = w_in_b[:, 3 * D_ATT:n_qkvc]
    w_gates = w_in_b[:, n_qkvc:]
    w_rows = jnp.concatenate([w_q, w_v, w_conv3], axis=1)
    w_cols = jnp.concatenate([w_k, w_v], axis=1).T
    gmix = norm_mix[0][None, :]
    gffn = norm_ffn[0][None, :]
    gfin = norm_final[None, :]
    cw = conv_w[0]
    bias = sb_bias[0]
    ffn_w = (w_att_out[0].astype(BF16), w_conv_out[0].astype(BF16), w_o[0].astype(BF16), gffn,
             w_gate[0].astype(BF16), w_up[0].astype(BF16), w_down[0].astype(BF16), gfin)
    u2 = _suffix_cols()

    zero_halo = jnp.zeros((CONV_K - 1, D_CONV), F32)
    x_meta = jnp.pad(meta_tokens, ((KBLK - N_META, 0), (0, 0)))[None]
    _, k2m, v2m, ktm_f, vtm_f, _, u_meta = _proj_call(x_meta, gmix, w_rows, w_cols, cw, zero_halo, KBLK)
    ktm_f, vtm_f = ktm_f[:, :, KBLK - N_META:], vtm_f[:, :, KBLK - N_META:]

    q_b, k2, v2, kt_f, vt_f, yc, conv_p = _proj_call(x_prompt, gmix, w_rows, w_cols, cw, u_meta[0], 512)
    o_att = _attn_call(bias * LOG2E, q_b, k2, v2, k2m[0, 0], v2m[0, 0], u2, tq=512)
    m_rows = n_prompt * seq
    y_p = _merge_ffn_call(x_prompt.reshape(m_rows, D_MODEL), o_att.reshape(m_rows, D_ATT),
                          yc.reshape(m_rows, D_CONV), gmix, w_gates, *ffn_w, tm=512)

    st = state_conv[0]
    q_s, k_s, v_s, yc_s, u_s = _sample_proj_call(x_sample[:, 0], gmix, w_in_b[:, :n_qkvc], cw, st[:, 0], st[:, 1])
    qb = jnp.broadcast_to(q_s[:, :, None], (n_dec, D_ATT, LANES))
    bias_b = jnp.broadcast_to(bias[:, None], (SB_HEADS, LANES))
    o_s = _decode_call(page_table, qb, bias_b, u2, jnp.ones((8, LANES), BF16),
                       _pages_as_rows(cache_k[0]), _pages_as_rows(cache_v[0]), 16)
    y_s = _merge_ffn_call(x_sample[:, 0], o_s[:, 0], yc_s, gmix, w_gates, *ffn_w, tm=n_dec)

    return (
        y_p.reshape(n_prompt, seq, D_MODEL),
        y_s.reshape(n_dec, 1, D_MODEL),
        _state_from_rows(ktm_f, kt_f),
        _state_from_rows(vtm_f, vt_f),
        conv_p[None],
        k_s.reshape(1, n_dec, 1, SB_HEADS, HEAD_DIM),
        v_s.reshape(1, n_dec, 1, SB_HEADS, HEAD_DIM),
        jnp.stack([st[:, 1], u_s], axis=1)[None],
    )
```

```python
import functools

import jax
import jax.numpy as jnp
import numpy as np
from jax import lax
from jax.experimental import pallas as pl
from jax.experimental.pallas import tpu as pltpu

F32 = jnp.float32
BF16 = jnp.bfloat16

D_MODEL = 1024
N_META = 16
SB_HEADS = 8
HEAD_DIM = 64
D_ATT = SB_HEADS * HEAD_DIM
D_CONV = 512
CONV_K = 3
EPS = 1e-6
PAGE = 128

LANES = 128
MXU_DIM = 256
KBLK = 128
N_PAIR = D_ATT // LANES
BIAS_LIMBS = 3
DEC_CHUNK = 8
DEC_SLOTS = 3
VMEM_LIMIT = 60 * 1024 * 1024
LOG2E = 1.4426950408889634

_NT = (((1,), (1,)), ((), ()))


def _rms(x, g):
    return x * lax.rsqrt(jnp.mean(x * x, axis=-1, keepdims=True) + EPS) * g


def _sigmoid(x):
    return 1.0 / (1.0 + jnp.exp(-x))


def _log_sigmoid_pair(z):
    l = jnp.log(1.0 + jnp.exp(-jnp.abs(z)))
    lb = jnp.minimum(z, 0.0) - l
    return lb, lb - z


def _split_bf16(x):
    hi = x.astype(BF16)
    lo = (x - hi.astype(F32)).astype(BF16)
    return hi, lo


def _const_spec(shape):
    nd = len(shape)
    return pl.BlockSpec(shape, lambda *_: (0,) * nd, pipeline_mode=pl.Buffered(1))


def _proj_kernel(x_ref, g_ref, w_ref, cw_ref, halo_ref, fk_ref, fv_ref,
                 q_ref, ktb_ref, vb_ref, ktf_ref, vtf_ref, yc_ref, st_ref,
                 carry_ref, kcar_ref, vcar_ref, *, n_front):
    m = pl.program_id(1)
    tm = x_ref.shape[1]

    def put_state(ref, car_ref, rows):
        if n_front == 0:
            ref[0] = rows.T
            return
        ref[0] = jnp.concatenate([car_ref[...], rows[:tm - n_front]], axis=0).T
        car_ref[...] = rows[tm - n_front:]

    def tile():
        h = _rms(x_ref[0], g_ref[...]).astype(BF16)
        p = jnp.dot(h, w_ref[...], preferred_element_type=F32)
        q_ref[0] = (p[:, 0:D_ATT] * (HEAD_DIM ** -0.5 * LOG2E)).astype(BF16)
        k = p[:, D_ATT:2 * D_ATT]
        v = p[:, 2 * D_ATT:3 * D_ATT]
        put_state(ktf_ref, kcar_ref, k)
        put_state(vtf_ref, vcar_ref, v)
        ktb = k.T.astype(BF16)
        vb = v.astype(BF16)
        kc = vb_ref.shape[3] // 2
        zk = jnp.zeros((HEAD_DIM, kc), BF16)
        lane = lax.broadcasted_iota(jnp.int32, (kc, LANES), 1)
        zv = jnp.zeros((kc, LANES), BF16)
        for c in range(tm // kc):
            for pr in range(N_PAIR):
                kblk = ktb[pr * LANES:(pr + 1) * LANES, c * kc:(c + 1) * kc]
                ktb_ref[0, c, pr] = jnp.concatenate(
                    [jnp.concatenate([kblk[:HEAD_DIM], zk], axis=0),
                     jnp.concatenate([zk, kblk[HEAD_DIM:]], axis=0)], axis=1)
                vblk = vb[c * kc:(c + 1) * kc, pr * LANES:(pr + 1) * LANES]
                vb_ref[0, c, pr] = jnp.concatenate(
                    [jnp.where(lane < HEAD_DIM, vblk, zv), jnp.where(lane >= HEAD_DIM, vblk, zv)], axis=0)

        bg = p[:, 3 * D_ATT:3 * D_ATT + D_CONV]
        u = p[:, 3 * D_ATT + D_CONV:3 * D_ATT + 2 * D_CONV] * p[:, 3 * D_ATT + 2 * D_CONV:3 * D_ATT + 3 * D_CONV]
        prev2 = carry_ref[0:1, :]
        prev1 = carry_ref[1:2, :]
        row = lax.broadcasted_iota(jnp.int32, u.shape, 0)
        u1 = jnp.where(row == 0, prev1, pltpu.roll(u, 1, 0))
        u2 = jnp.where(row == 0, prev2, jnp.where(row == 1, prev1, pltpu.roll(u, 2, 0)))
        cw = cw_ref[...]
        yc = bg * (cw[0:1] * u2 + cw[1:2] * u1 + cw[2:3] * u)
        yc_ref[0] = yc.astype(BF16)
        tail = u[tm - 2:tm, :]
        carry_ref[0:2, :] = tail
        st_ref[0] = tail

    @pl.when(m == 0)
    def _():
        carry_ref[0:2, :] = halo_ref[...]
        if n_front:
            kcar_ref[...] = fk_ref[...].T[LANES - n_front:, :]
            vcar_ref[...] = fv_ref[...].T[LANES - n_front:, :]

    if n_front == 0:
        tile()
    else:
        n_tiles = pl.num_programs(1) - 1
        pl.when(m < n_tiles)(tile)

        @pl.when(m == n_tiles)
        def _():
            pad = jnp.zeros((LANES - n_front, D_ATT), F32)
            ktf_ref[0, :, :LANES] = jnp.concatenate([kcar_ref[...], pad], axis=0).T
            vtf_ref[0, :, :LANES] = jnp.concatenate([vcar_ref[...], pad], axis=0).T


def _proj_call(x, g, w, cw, halo, front_k, front_v, n_front, tm):
    b, t, _ = x.shape
    kc = min(KBLK, tm)
    n_tiles = t // tm
    tile_idx = (lambda j: jnp.minimum(j, n_tiles - 1)) if n_front else (lambda j: j)
    row_spec = lambda width: pl.BlockSpec((1, tm, width), lambda i, j: (i, tile_idx(j), 0))
    col_spec = pl.BlockSpec((1, D_ATT, tm), lambda i, j: (i, 0, j))
    outs = (
        jax.ShapeDtypeStruct((b, t, D_ATT), BF16),
        jax.ShapeDtypeStruct((b, t // kc, N_PAIR, LANES, 2 * kc), BF16),
        jax.ShapeDtypeStruct((b, t // kc, N_PAIR, 2 * kc, LANES), BF16),
        jax.ShapeDtypeStruct((b, D_ATT, n_front + t), F32),
        jax.ShapeDtypeStruct((b, D_ATT, n_front + t), F32),
        jax.ShapeDtypeStruct((b, t, D_CONV), BF16),
        jax.ShapeDtypeStruct((b, CONV_K - 1, D_CONV), F32),
    )
    return pl.pallas_call(
        functools.partial(_proj_kernel, n_front=n_front),
        grid=(b, n_tiles + (1 if n_front else 0)),
        in_specs=[
            row_spec(D_MODEL),
            _const_spec((1, D_MODEL)),
            _const_spec(w.shape),
            _const_spec((CONV_K, D_CONV)),
            _const_spec((CONV_K - 1, D_CONV)),
            _const_spec((D_ATT, LANES)),
            _const_spec((D_ATT, LANES)),
        ],
        out_specs=(
            row_spec(D_ATT),
            pl.BlockSpec((1, tm // kc, N_PAIR, LANES, 2 * kc), lambda i, j: (i, tile_idx(j), 0, 0, 0)),
            pl.BlockSpec((1, tm // kc, N_PAIR, 2 * kc, LANES), lambda i, j: (i, tile_idx(j), 0, 0, 0)),
            col_spec, col_spec,
            row_spec(D_CONV),
            pl.BlockSpec((1, CONV_K - 1, D_CONV), lambda i, j: (i, 0, 0)),
        ),
        out_shape=outs,
        scratch_shapes=[pltpu.VMEM((8, D_CONV), F32), pltpu.VMEM((max(n_front, 8), D_ATT), F32),
                        pltpu.VMEM((max(n_front, 8), D_ATT), F32)],
        compiler_params=pltpu.CompilerParams(
            dimension_semantics=("arbitrary", "arbitrary"), vmem_limit_bytes=VMEM_LIMIT),
        name="proj_conv",
    )(x, g, w, cw, halo, front_k, front_v)


def _sb_logits(z2, u2, mask):
    ys, tots = [], []
    for hh in range(2):
        z = z2[:, hh * KBLK:(hh + 1) * KBLK]
        neg_abs = pltpu.bitcast(pltpu.bitcast(z, jnp.uint32) | jnp.uint32(0x80000000), F32)
        p = jnp.maximum(z, 0.0) + jnp.log(1.0 + jnp.exp2(neg_abs)) * LOG2E
        if mask is not None:
            p = jnp.where(mask, p, 0.0)
        hi = pltpu.bitcast(pltpu.bitcast(p, jnp.uint32) & jnp.uint32(0xFFFF0000), F32)
        lo = p - hi
        s_ext = jnp.dot(jnp.concatenate([hi, lo], axis=1), u2, preferred_element_type=F32)
        ys.append(z + s_ext[:, :KBLK])
        tots.append(s_ext[:, KBLK:])
    return ys, tots


def _sb_weights(ys, tots, c_ref, rows, mask):
    ws = []
    for hh in range(2):
        c = c_ref[hh, rows, :]
        w = jnp.exp2(ys[hh] + c)
        if mask is not None:
            w = jnp.where(mask, w, 0.0)
        if tots is not None:
            c_ref[hh, rows, :] = c + tots[hh]
        ws.append(w.astype(BF16))
    return jnp.concatenate(ws, axis=1)


class _QueryTileSweep:
    def __init__(self, qt, q_ref, o_ref, shared, scratch):
        self.qt, self.q_ref, self.o_ref = qt, q_ref, o_ref
        self.kt_ref, self.v_ref, self.ktm_ref, self.vm_ref, self.kb_ref, self.kbm_ref, self.u2 = shared
        self.qx_ref, self.c_ref, self.acc_ref, self.zn_ref, self.wl_ref, self.ym_ref = scratch
        self.tq = q_ref.shape[0]
        self.nsub = self.tq // KBLK
        self.full = slice(0, self.tq)

    def scores(self, rows, k2, kb):
        return jnp.dot(self.qx_ref[rows, :], jnp.concatenate([k2, kb], axis=0), preferred_element_type=F32)

    def add_pv(self, rows, w2, v2):
        self.acc_ref[rows, :] += jnp.dot(w2, v2, preferred_element_type=F32)

    def overlapping_blocks(self):
        tq, nsub, qt = self.tq, self.nsub, self.qt
        self.qx_ref[:, :LANES] = self.q_ref[...]
        lane = lax.broadcasted_iota(jnp.int32, (tq, LANES), 1)
        self.qx_ref[:, LANES:] = jnp.where(lane < BIAS_LIMBS, 1.0, 0.0).astype(BF16)
        self.c_ref[...] = jnp.zeros_like(self.c_ref)
        self.acc_ref[...] = jnp.zeros_like(self.acc_ref)
        for jj in reversed(range(nsub)):
            j = qt * nsub + jj
            rows = slice(jj * KBLK, tq)
            row = lax.broadcasted_iota(jnp.int32, (tq - jj * KBLK, KBLK), 0)
            col = lax.broadcasted_iota(jnp.int32, (tq - jj * KBLK, KBLK), 1)
            mask = col < row
            z2 = self.scores(rows, self.kt_ref[j], self.kb_ref[...])
            w2 = _sb_weights(*_sb_logits(z2, self.u2, mask), self.c_ref, rows, mask)
            if jj > 0:
                self.add_pv(rows, w2, self.v_ref[j])
            else:
                self.wl_ref[...] = w2
            yield
        self.zn_ref[...] = self.scores(self.full, self.kt_ref[jnp.maximum(qt * nsub - 1, 0)], self.kb_ref[...])
        ys_meta, _ = _sb_logits(self.scores(self.full, self.ktm_ref[...], self.kbm_ref[...]), self.u2, None)
        self.ym_ref[...] = jnp.concatenate(ys_meta, axis=1)
        yield

    def earlier_blocks(self):
        nsub, qt, full = self.nsub, self.qt, self.full

        def body(i, j_pending):
            self.add_pv(full, self.wl_ref[...], self.v_ref[j_pending])
            j0 = (qt - i) * nsub - 1
            for jj in range(nsub):
                j = j0 - jj
                z2 = self.zn_ref[...] if jj == 0 else self.scores(full, self.kt_ref[j], self.kb_ref[...])
                w2 = _sb_weights(*_sb_logits(z2, self.u2, None), self.c_ref, full, None)
                if jj < nsub - 1:
                    self.add_pv(full, w2, self.v_ref[j])
                else:
                    self.wl_ref[...] = w2
            self.zn_ref[...] = self.scores(full, self.kt_ref[jnp.maximum(j0 - nsub, 0)], self.kb_ref[...])
            return j0 - (nsub - 1)

        self.j_pending = lax.fori_loop(0, qt, body, qt * nsub)

    def meta_block(self):
        self.add_pv(self.full, self.wl_ref[...], self.v_ref[self.j_pending])
        ys_meta = [self.ym_ref[:, hh * KBLK:(hh + 1) * KBLK] for hh in range(2)]
        self.add_pv(self.full, _sb_weights(ys_meta, None, self.c_ref, self.full, None), self.vm_ref[...])
        self.o_ref[...] = self.acc_ref[...].astype(self.o_ref.dtype)


def _attn_kernel(qa_ref, qb_ref, kt_ref, v_ref, ktm_ref, vm_ref, kb_ref, kbm_ref, u2_ref, o_ref, *scratch):
    j = pl.program_id(2)
    n_tiles = o_ref.shape[0]
    shared = (kt_ref, v_ref, ktm_ref, vm_ref, kb_ref, kbm_ref, u2_ref[...])
    tiles = [
        _QueryTileSweep(j, qa_ref, o_ref.at[j], shared, [s.at[0] for s in scratch]),
        _QueryTileSweep(n_tiles - 1 - j, qb_ref, o_ref.at[n_tiles - 1 - j], shared, [s.at[1] for s in scratch]),
    ]
    for _ in zip(*[t.overlapping_blocks() for t in tiles]):
        pass
    for t in tiles:
        t.earlier_blocks()
    for t in tiles:
        t.meta_block()


def _bias_rows(bias2, n_pad):
    limbs = []
    r = bias2
    for _ in range(BIAS_LIMBS):
        limb = r.astype(BF16)
        limbs.append(limb)
        r = r - limb.astype(F32)
    rows = jnp.repeat(jnp.stack(limbs).reshape(BIAS_LIMBS, N_PAIR, 2), KBLK, axis=2)
    key = jnp.arange(2 * KBLK) % KBLK
    pad_row = jnp.where(jnp.arange(BIAS_LIMBS)[:, None, None] == 0, -1e30, 0.0).astype(BF16)
    rows = jnp.where(key[None, None, :] < n_pad, pad_row, rows)
    out = jnp.zeros((N_PAIR, LANES, 2 * KBLK), BF16)
    return out.at[:, :BIAS_LIMBS, :].set(jnp.transpose(rows, (1, 0, 2)))


def _attn_call(bias2, q, k2, v2, k2m, v2m, u2, tq):
    b, t, _ = q.shape
    nkb = t // KBLK
    n_tiles = t // tq
    assert n_tiles % 2 == 0
    pair_block = lambda rows, cols: pl.BlockSpec((None, rows, cols), lambda i, p, j: (p, 0, 0))
    tile_a = pl.BlockSpec((None, tq, LANES), lambda i, p, j: (i, j, p))
    tile_b = pl.BlockSpec((None, tq, LANES), lambda i, p, j: (i, n_tiles - 1 - j, p))
    return pl.pallas_call(
        _attn_kernel,
        grid=(b, N_PAIR, n_tiles // 2),
        in_specs=[
            tile_a, tile_b,
            pl.BlockSpec((None, nkb, None, LANES, 2 * KBLK), lambda i, p, j: (i, 0, p, 0, 0)),
            pl.BlockSpec((None, nkb, None, 2 * KBLK, LANES), lambda i, p, j: (i, 0, p, 0, 0)),
            pair_block(LANES, 2 * KBLK),
            pair_block(2 * KBLK, LANES),
            pair_block(LANES, 2 * KBLK),
            pair_block(LANES, 2 * KBLK),
            pl.BlockSpec((2 * KBLK, 2 * KBLK), lambda i, p, j: (0, 0)),
        ],
        out_specs=pl.BlockSpec((None, n_tiles, tq, LANES), lambda i, p, j: (i, 0, 0, p)),
        out_shape=jax.ShapeDtypeStruct((b, n_tiles, tq, D_ATT), BF16),
        scratch_shapes=[pltpu.VMEM((2, tq, 2 * LANES), BF16), pltpu.VMEM((2, 2, tq, KBLK), F32),
                        pltpu.VMEM((2, tq, LANES), F32), pltpu.VMEM((2, tq, 2 * KBLK), F32),
                        pltpu.VMEM((2, tq, 2 * KBLK), BF16), pltpu.VMEM((2, tq, 2 * KBLK), F32)],
        compiler_params=pltpu.CompilerParams(
            dimension_semantics=("arbitrary", "arbitrary", "arbitrary"), vmem_limit_bytes=VMEM_LIMIT),
        name="sb_attn",
    )(q, q, k2, v2, k2m, v2m, _bias_rows(bias2, 0), _bias_rows(bias2, KBLK - N_META), u2)


def _ffn_steps(x_ref, o_ref, yc_ref, gmix_ref, wg_ref, wa_ref, wc_ref, wo_ref,
               gffn_ref, wgate_ref, wup_ref, wdown_ref, gfin_ref, y_ref, ff_chunks):
    s = {}

    def gates():
        s["x"] = x_ref[...]
        h = _rms(s["x"], gmix_ref[...]).astype(BF16)
        s["g"] = jnp.dot(h, wg_ref[...], preferred_element_type=F32)

    def merge():
        ya = jnp.dot(o_ref[...], wa_ref[...], preferred_element_type=F32)
        yb = jnp.dot(yc_ref[...], wc_ref[...], preferred_element_type=F32)
        g = s.pop("g")
        mrg = _sigmoid(g[:, :D_MODEL]) * ya + _sigmoid(g[:, D_MODEL:]) * yb
        x1 = s.pop("x") + jnp.dot(mrg.astype(BF16), wo_ref[...], preferred_element_type=F32)
        s["h2"] = _rms(x1, gffn_ref[...]).astype(BF16)
        s["acc"] = x1

    def gate(c0, c1):
        s["a"] = jnp.dot(s["h2"], wgate_ref[:, c0:c1], preferred_element_type=F32)

    def up(c0, c1):
        a = s.pop("a")
        b = jnp.dot(s["h2"], wup_ref[:, c0:c1], preferred_element_type=F32)
        s["act"] = (a * _sigmoid(a) * b).astype(BF16)

    def down(c0, c1):
        s["acc"] = s["acc"] + jnp.dot(s.pop("act"), wdown_ref[c0:c1, :], preferred_element_type=F32)

    def out():
        y_ref[...] = _rms(s["acc"], gfin_ref[...])

    steps = [gates, merge]
    for c0, c1 in ff_chunks:
        steps += [functools.partial(f, c0, c1) for f in (gate, up, down)]
    return steps + [out]


def _merge_ffn_kernel(*refs, ff_chunks):
    for step in _ffn_steps(*refs, ff_chunks):
        step()


def _suffix_sums_lanes(x):
    lane = lax.broadcasted_iota(jnp.int32, x.shape, 1)
    y = x
    k = 1
    while k < LANES:
        shifted = pltpu.roll(y, LANES - k, axis=1)
        y = y + jnp.where(lane < LANES - k, shifted, 0.0)
        k *= 2
    return y


def _decode_pages(k_pages, v_pages, qb_ref, bias_ref, c_ref, acc_ref, w_ref):
    n = k_pages.shape[0]
    zs = []
    for i in range(n):
        rows = []
        for h in range(SB_HEADS):
            hs = slice(h * HEAD_DIM, (h + 1) * HEAD_DIM)
            rows.append(jnp.sum(k_pages[i, hs, :] * qb_ref[hs, :], axis=0, keepdims=True))
        zs.append(jnp.concatenate(rows, axis=0) + bias_ref[...])
    z = jnp.concatenate(zs, axis=0)
    lb, l1m = _log_sigmoid_pair(z)
    later = _suffix_sums_lanes(l1m) - l1m
    tot = jnp.sum(l1m, axis=1, keepdims=True)
    c = c_ref[...]
    for i in range(n):
        rows = slice(i * SB_HEADS, (i + 1) * SB_HEADS)
        w_ref[rows, :] = jnp.exp(lb[rows] + later[rows] + c)
        c = c + tot[rows]
    c_ref[...] = c
    for h in range(SB_HEADS):
        hs = slice(h * HEAD_DIM, (h + 1) * HEAD_DIM)
        a = acc_ref[hs, :]
        for i in range(n):
            r = i * SB_HEADS + h
            a = a + v_pages[i, hs, :] * w_ref[r:r + 1, :]
        acc_ref[hs, :] = a


def _lane_sums_row(acc, ones):
    a0 = acc.astype(BF16)
    r1 = acc - a0.astype(F32)
    a1 = r1.astype(BF16)
    a2 = (r1 - a1.astype(F32)).astype(BF16)
    tot = (lax.dot_general(ones, a0, _NT, preferred_element_type=F32)
           + lax.dot_general(ones, a1, _NT, preferred_element_type=F32)
           + lax.dot_general(ones, a2, _NT, preferred_element_type=F32))
    return tot[0:1]


def _merge_ffn_decode_kernel(pt_ref, *refs, ff_chunks, n_pages):
    ffn_refs, (qb_ref, bias_ref, ones_ref, kc_hbm, vc_hbm) = refs[:13], refs[13:18]
    y_ref, os_ref, kbuf, vbuf, sem, c_ref, acc_ref, w_ref = refs[18:]
    seq = pl.program_id(0)
    n_seq = pl.num_programs(0)
    n_chunks = n_pages // DEC_CHUNK
    ahead = DEC_SLOTS - 1

    def copies(g):
        sq = jnp.minimum(g // n_chunks, n_seq - 1)
        chunk = g % n_chunks
        slot = g % DEC_SLOTS
        out = []
        for i in range(DEC_CHUNK):
            page = pt_ref[sq * n_pages + (n_pages - 1 - (chunk * DEC_CHUNK + i))]
            out.append(pltpu.make_async_copy(kc_hbm.at[page], kbuf.at[slot, i], sem.at[0, slot]))
            out.append(pltpu.make_async_copy(vc_hbm.at[page], vbuf.at[slot, i], sem.at[1, slot]))
        return out

    g0 = seq * n_chunks

    @pl.when(seq == 0)
    def _():
        for g in range(ahead):
            for cp in copies(g):
                cp.start()

    c_ref[...] = jnp.zeros_like(c_ref)
    acc_ref[...] = jnp.zeros_like(acc_ref)
    steps = _ffn_steps(*ffn_refs, y_ref, ff_chunks)
    assert len(steps) == n_chunks + 1
    for chunk in range(n_chunks):
        g = g0 + chunk
        for cp in copies(g + ahead):
            cp.start()
        for cp in copies(g):
            cp.wait()
        slot = g % DEC_SLOTS
        _decode_pages(kbuf.at[slot], vbuf.at[slot], qb_ref, bias_ref, c_ref, acc_ref, w_ref)
        steps[chunk]()
    steps[n_chunks]()
    os_ref[0] = _lane_sums_row(acc_ref[...], ones_ref[...]).astype(os_ref.dtype)

    @pl.when(seq == n_seq - 1)
    def _():
        for k in range(ahead):
            for cp in copies(n_seq * n_chunks + k):
                cp.wait()


def _ff_chunks(d_ff):
    cut = pl.cdiv(pl.cdiv(d_ff, MXU_DIM), 2) * MXU_DIM
    return ((0, cut), (cut, d_ff))


def _merge_ffn_call(x, o, yc, gmix, wg, wa, wc, wo, gffn, wgate, wup, wdown, gfin, tm):
    m = x.shape[0]
    d_ff = wgate.shape[1]
    row_spec = lambda width: pl.BlockSpec((tm, width), lambda i: (i, 0))
    return pl.pallas_call(
        functools.partial(_merge_ffn_kernel, ff_chunks=_ff_chunks(d_ff)),
        grid=(m // tm,),
        in_specs=[
            row_spec(D_MODEL), row_spec(D_ATT), row_spec(D_CONV),
            _const_spec((1, D_MODEL)),
            _const_spec(wg.shape), _const_spec(wa.shape), _const_spec(wc.shape), _const_spec(wo.shape),
            _const_spec((1, D_MODEL)),
            _const_spec(wgate.shape), _const_spec(wup.shape), _const_spec(wdown.shape),
            _const_spec((1, D_MODEL)),
        ],
        out_specs=row_spec(D_MODEL),
        out_shape=jax.ShapeDtypeStruct((m, D_MODEL), F32),
        compiler_params=pltpu.CompilerParams(
            dimension_semantics=("arbitrary",), vmem_limit_bytes=VMEM_LIMIT),
        name="merge_ffn",
    )(x, o, yc, gmix, wg, wa, wc, wo, gffn, wgate, wup, wdown, gfin)


def _merge_ffn_decode_call(page_table, ffn_args, qb, bias_b, ones, cache_kt, cache_vt, tm):
    x, wgate = ffn_args[0], ffn_args[9]
    m = x.shape[0]
    n_dec, n_pages = page_table.shape
    assert m // tm == n_dec
    row_spec = lambda width: pl.BlockSpec((tm, width), lambda i, pt: (i, 0))
    const = lambda a: _const_spec(a.shape)
    any_spec = pl.BlockSpec(memory_space=pl.ANY)
    grid_spec = pltpu.PrefetchScalarGridSpec(
        num_scalar_prefetch=1,
        grid=(n_dec,),
        in_specs=[row_spec(D_MODEL), row_spec(D_ATT), row_spec(D_CONV)] + [const(a) for a in ffn_args[3:]] + [
            pl.BlockSpec((None, D_ATT, LANES), lambda i, pt: (i, 0, 0)),
            const(bias_b), const(ones), any_spec, any_spec,
        ],
        out_specs=(row_spec(D_MODEL), pl.BlockSpec((1, 1, D_ATT), lambda i, pt: (i, 0, 0))),
        scratch_shapes=[
            pltpu.VMEM((DEC_SLOTS, DEC_CHUNK, D_ATT, PAGE), F32),
            pltpu.VMEM((DEC_SLOTS, DEC_CHUNK, D_ATT, PAGE), F32),
            pltpu.SemaphoreType.DMA((2, DEC_SLOTS)),
            pltpu.VMEM((SB_HEADS, LANES), F32),
            pltpu.VMEM((D_ATT, PAGE), F32),
            pltpu.VMEM((DEC_CHUNK * SB_HEADS, PAGE), F32),
        ],
    )
    return pl.pallas_call(
        functools.partial(_merge_ffn_decode_kernel, ff_chunks=_ff_chunks(wgate.shape[1]), n_pages=n_pages),
        grid_spec=grid_spec,
        out_shape=(jax.ShapeDtypeStruct((m, D_MODEL), F32), jax.ShapeDtypeStruct((n_dec, 1, D_ATT), BF16)),
        compiler_params=pltpu.CompilerParams(
            dimension_semantics=("arbitrary",), vmem_limit_bytes=VMEM_LIMIT),
        name="merge_ffn_decode",
    )(page_table.reshape(-1), *ffn_args, qb, bias_b, ones, cache_kt, cache_vt)


def _sample_proj_kernel(x_ref, g_ref, w_ref, cw_ref, s0_ref, s1_ref, q_ref, kf_ref, vf_ref, yc_ref, u_ref):
    h = _rms(x_ref[...], g_ref[...]).astype(BF16)
    p = jnp.dot(h, w_ref[...], preferred_element_type=F32)
    q_ref[...] = p[:, 0:D_ATT] * (HEAD_DIM ** -0.5)
    kf_ref[...] = p[:, D_ATT:2 * D_ATT]
    vf_ref[...] = p[:, 2 * D_ATT:3 * D_ATT]
    bg = p[:, 3 * D_ATT:3 * D_ATT + D_CONV]
    u = p[:, 3 * D_ATT + D_CONV:3 * D_ATT + 2 * D_CONV] * p[:, 3 * D_ATT + 2 * D_CONV:3 * D_ATT + 3 * D_CONV]
    cw = cw_ref[...]
    yc_ref[...] = (bg * (cw[0:1] * s0_ref[...] + cw[1:2] * s1_ref[...] + cw[2:3] * u)).astype(BF16)
    u_ref[...] = u


def _sample_proj_call(x, g, w, cw, s0, s1):
    n = x.shape[0]
    full = lambda shape: pl.BlockSpec(shape, lambda i: (0,) * len(shape))
    outs = (
        jax.ShapeDtypeStruct((n, D_ATT), F32),
        jax.ShapeDtypeStruct((n, D_ATT), F32),
        jax.ShapeDtypeStruct((n, D_ATT), F32),
        jax.ShapeDtypeStruct((n, D_CONV), BF16),
        jax.ShapeDtypeStruct((n, D_CONV), F32),
    )
    return pl.pallas_call(
        _sample_proj_kernel,
        grid=(1,),
        in_specs=[full(x.shape), full(g.shape), full(w.shape), full(cw.shape), full(s0.shape), full(s1.shape)],
        out_specs=tuple(full(o.shape) for o in outs),
        out_shape=outs,
        compiler_params=pltpu.CompilerParams(vmem_limit_bytes=VMEM_LIMIT),
        name="sample_proj",
    )(x, g, w, cw, s0, s1)


def _suffix_cols(inclusive, sign, dtype):
    j = np.arange(KBLK)[:, None]
    s = np.arange(KBLK)[None, :]
    later = (j >= s) if inclusive else (j > s)
    blk = sign * np.concatenate([later.astype(np.float32), np.ones((KBLK, KBLK), np.float32)], axis=1)
    return jnp.asarray(np.concatenate([blk, blk], axis=0), dtype=dtype)


def _pages_as_rows(cache):
    n_phys = cache.shape[0]
    return jnp.transpose(cache, (0, 2, 3, 1)).reshape(n_phys, D_ATT, PAGE)


def _state_from_rows(rows_t):
    b, _, n_pos = rows_t.shape
    return jnp.transpose(rows_t.reshape(b, SB_HEADS, HEAD_DIM, n_pos), (0, 3, 1, 2))[None]


def kernel(x_prompt, x_sample, cache_k, cache_v, state_conv, page_table, meta_tokens, norm_mix, w_in,
           sb_bias, conv_w, w_att_out, w_conv_out, w_o, norm_ffn, w_gate, w_up, w_down, norm_final):
    assert w_in.shape[0] == 1, "single-layer step: the meta rows' block output never reaches a returned array"
    n_prompt, seq, _ = x_prompt.shape
    n_dec, t_dec, _ = x_sample.shape
    assert t_dec == 1

    w_in_b = w_in[0].astype(BF16)
    n_qkvc = 3 * D_ATT + 3 * D_CONV
    w_qkvc = w_in_b[:, :n_qkvc]
    w_gates = w_in_b[:, n_qkvc:]
    gmix = norm_mix[0][None, :]
    gffn = norm_ffn[0][None, :]
    gfin = norm_final[None, :]
    cw = conv_w[0]
    bias = sb_bias[0]
    ffn_w = (w_att_out[0].astype(BF16), w_conv_out[0].astype(BF16), w_o[0].astype(BF16), gffn,
             w_gate[0].astype(BF16), w_up[0].astype(BF16), w_down[0].astype(BF16), gfin)

    zero_halo = jnp.zeros((CONV_K - 1, D_CONV), F32)
    x_meta = jnp.pad(meta_tokens, ((KBLK - N_META, 0), (0, 0)))[None]
    no_front = jnp.zeros((D_ATT, LANES), F32)
    _, k2m, v2m, ktm_f, vtm_f, _, u_meta = _proj_call(
        x_meta, gmix, w_qkvc, cw, zero_halo, no_front, no_front, 0, KBLK)

    q_b, k2, v2, kt_f, vt_f, yc, conv_p = _proj_call(
        x_prompt, gmix, w_qkvc, cw, u_meta[0], ktm_f[0], vtm_f[0], N_META, 512)
    o_att = _attn_call(bias * LOG2E, q_b, k2, v2, k2m[0, 0], v2m[0, 0],
                       _suffix_cols(inclusive=True, sign=-1.0, dtype=F32), tq=512)

    st = state_conv[0]
    q_s, k_s, v_s, yc_s, u_s = _sample_proj_call(x_sample[:, 0], gmix, w_qkvc, cw, st[:, 0], st[:, 1])
    qb = jnp.broadcast_to(q_s[:, :, None], (n_dec, D_ATT, LANES))
    bias_b = jnp.broadcast_to(bias[:, None], (SB_HEADS, LANES))
    m_rows = n_prompt * seq
    prompt_rows = (x_prompt.reshape(m_rows, D_MODEL), o_att.reshape(m_rows, D_ATT), yc.reshape(m_rows, D_CONV))
    y_p, o_s = _merge_ffn_decode_call(
        page_table, prompt_rows + (gmix, w_gates) + ffn_w, qb, bias_b, jnp.ones((8, LANES), BF16),
        _pages_as_rows(cache_k[0]), _pages_as_rows(cache_v[0]), tm=m_rows // n_dec)
    y_s = _merge_ffn_call(x_sample[:, 0], o_s[:, 0], yc_s, gmix, w_gates, *ffn_w, tm=n_dec)

    return (
        y_p.reshape(n_prompt, seq, D_MODEL),
        y_s.reshape(n_dec, 1, D_MODEL),
        _state_from_rows(kt_f),
        _state_from_rows(vt_f),
        conv_p[None],
        k_s.reshape(1, n_dec, 1, SB_HEADS, HEAD_DIM),
        v_s.reshape(1, n_dec, 1, SB_HEADS, HEAD_DIM),
        jnp.stack([st[:, 1], u_s], axis=1)[None],
    )
```

```python
import functools

import jax
import jax.numpy as jnp
import numpy as np
from jax import lax
from jax.experimental import pallas as pl
from jax.experimental.pallas import tpu as pltpu

F32 = jnp.float32
BF16 = jnp.bfloat16

D_MODEL = 1024
N_META = 16
SB_HEADS = 8
HEAD_DIM = 64
D_ATT = SB_HEADS * HEAD_DIM
D_CONV = 512
CONV_K = 3
EPS = 1e-6
PAGE = 128

LANES = 128
MXU_DIM = 256
KBLK = 128
N_PAIR = D_ATT // LANES
BIAS_LIMBS = 3
DEC_CHUNK = 8
DEC_SLOTS = 3
VMEM_LIMIT = 60 * 1024 * 1024
LOG2E = 1.4426950408889634

_NT = (((1,), (1,)), ((), ()))


def _rms(x, g):
    return x * lax.rsqrt(jnp.mean(x * x, axis=-1, keepdims=True) + EPS) * g


def _sigmoid(x):
    return 1.0 / (1.0 + jnp.exp(-x))


def _log_sigmoid_pair(z):
    l = jnp.log(1.0 + jnp.exp(-jnp.abs(z)))
    lb = jnp.minimum(z, 0.0) - l
    return lb, lb - z


def _split_bf16(x):
    hi = x.astype(BF16)
    lo = (x - hi.astype(F32)).astype(BF16)
    return hi, lo


def _const_spec(shape):
    nd = len(shape)
    return pl.BlockSpec(shape, lambda *_: (0,) * nd, pipeline_mode=pl.Buffered(1))


def _proj_kernel(x_ref, g_ref, w_ref, cw_ref, halo_ref, fk_ref, fv_ref,
                 q_ref, ktb_ref, vb_ref, ktf_ref, vtf_ref, yc_ref, st_ref,
                 carry_ref, kcar_ref, vcar_ref, *, n_front):
    m = pl.program_id(1)
    tm = x_ref.shape[1]

    def put_state(ref, car_ref, rows):
        if n_front == 0:
            ref[0] = rows.T
            return
        ref[0] = jnp.concatenate([car_ref[...], rows[:tm - n_front]], axis=0).T
        car_ref[...] = rows[tm - n_front:]

    def tile():
        h = _rms(x_ref[0], g_ref[...]).astype(BF16)
        p = jnp.dot(h, w_ref[...], preferred_element_type=F32)
        q_ref[0] = (p[:, 0:D_ATT] * (HEAD_DIM ** -0.5 * LOG2E)).astype(BF16)
        k = p[:, D_ATT:2 * D_ATT]
        v = p[:, 2 * D_ATT:3 * D_ATT]
        put_state(ktf_ref, kcar_ref, k)
        put_state(vtf_ref, vcar_ref, v)
        ktb = k.T.astype(BF16)
        for c in range(tm // KBLK):
            ktb_ref[0, c] = ktb[:, c * KBLK:(c + 1) * KBLK]
        vb_ref[0] = v.astype(BF16)

        bg = p[:, 3 * D_ATT:3 * D_ATT + D_CONV]
        u = p[:, 3 * D_ATT + D_CONV:3 * D_ATT + 2 * D_CONV] * p[:, 3 * D_ATT + 2 * D_CONV:3 * D_ATT + 3 * D_CONV]
        prev2 = carry_ref[0:1, :]
        prev1 = carry_ref[1:2, :]
        row = lax.broadcasted_iota(jnp.int32, u.shape, 0)
        u1 = jnp.where(row == 0, prev1, pltpu.roll(u, 1, 0))
        u2 = jnp.where(row == 0, prev2, jnp.where(row == 1, prev1, pltpu.roll(u, 2, 0)))
        cw = cw_ref[...]
        yc = bg * (cw[0:1] * u2 + cw[1:2] * u1 + cw[2:3] * u)
        yc_ref[0] = yc.astype(BF16)
        tail = u[tm - 2:tm, :]
        carry_ref[0:2, :] = tail
        st_ref[0] = tail

    @pl.when(m == 0)
    def _():
        carry_ref[0:2, :] = halo_ref[...]
        if n_front:
            kcar_ref[...] = fk_ref[...].T[LANES - n_front:, :]
            vcar_ref[...] = fv_ref[...].T[LANES - n_front:, :]

    if n_front == 0:
        tile()
    else:
        n_tiles = pl.num_programs(1) - 1
        pl.when(m < n_tiles)(tile)

        @pl.when(m == n_tiles)
        def _():
            pad = jnp.zeros((LANES - n_front, D_ATT), F32)
            ktf_ref[0, :, :LANES] = jnp.concatenate([kcar_ref[...], pad], axis=0).T
            vtf_ref[0, :, :LANES] = jnp.concatenate([vcar_ref[...], pad], axis=0).T


def _proj_call(x, g, w, cw, halo, front_k, front_v, n_front, tm):
    b, t, _ = x.shape
    assert tm % KBLK == 0
    n_tiles = t // tm
    tile_idx = (lambda j: jnp.minimum(j, n_tiles - 1)) if n_front else (lambda j: j)
    row_spec = lambda width: pl.BlockSpec((1, tm, width), lambda i, j: (i, tile_idx(j), 0))
    col_spec = pl.BlockSpec((1, D_ATT, tm), lambda i, j: (i, 0, j))
    outs = (
        jax.ShapeDtypeStruct((b, t, D_ATT), BF16),
        jax.ShapeDtypeStruct((b, t // KBLK, D_ATT, KBLK), BF16),
        jax.ShapeDtypeStruct((b, t, D_ATT), BF16),
        jax.ShapeDtypeStruct((b, D_ATT, n_front + t), F32),
        jax.ShapeDtypeStruct((b, D_ATT, n_front + t), F32),
        jax.ShapeDtypeStruct((b, t, D_CONV), BF16),
        jax.ShapeDtypeStruct((b, CONV_K - 1, D_CONV), F32),
    )
    return pl.pallas_call(
        functools.partial(_proj_kernel, n_front=n_front),
        grid=(b, n_tiles + (1 if n_front else 0)),
        in_specs=[
            row_spec(D_MODEL),
            _const_spec((1, D_MODEL)),
            _const_spec(w.shape),
            _const_spec((CONV_K, D_CONV)),
            _const_spec((CONV_K - 1, D_CONV)),
            _const_spec((D_ATT, LANES)),
            _const_spec((D_ATT, LANES)),
        ],
        out_specs=(
            row_spec(D_ATT),
            pl.BlockSpec((1, tm // KBLK, D_ATT, KBLK), lambda i, j: (i, tile_idx(j), 0, 0)),
            row_spec(D_ATT),
            col_spec, col_spec,
            row_spec(D_CONV),
            pl.BlockSpec((1, CONV_K - 1, D_CONV), lambda i, j: (i, 0, 0)),
        ),
        out_shape=outs,
        scratch_shapes=[pltpu.VMEM((8, D_CONV), F32), pltpu.VMEM((max(n_front, 8), D_ATT), F32),
                        pltpu.VMEM((max(n_front, 8), D_ATT), F32)],
        compiler_params=pltpu.CompilerParams(
            dimension_semantics=("arbitrary", "arbitrary"), vmem_limit_bytes=VMEM_LIMIT),
        name="proj_conv",
    )(x, g, w, cw, halo, front_k, front_v)


def _sb_logits(z2, u2, mask):
    ys, tots = [], []
    for hh in range(2):
        z = z2[:, hh * KBLK:(hh + 1) * KBLK]
        neg_abs = pltpu.bitcast(pltpu.bitcast(z, jnp.uint32) | jnp.uint32(0x80000000), F32)
        p = jnp.maximum(z, 0.0) + jnp.log(1.0 + jnp.exp2(neg_abs)) * LOG2E
        if mask is not None:
            p = jnp.where(mask, p, 0.0)
        hi = pltpu.bitcast(pltpu.bitcast(p, jnp.uint32) & jnp.uint32(0xFFFF0000), F32)
        lo = p - hi
        s_ext = jnp.dot(jnp.concatenate([hi, lo], axis=1), u2, preferred_element_type=F32)
        ys.append(z + s_ext[:, :KBLK])
        tots.append(s_ext[:, KBLK:])
    return ys, tots


def _sb_weights(ys, tots, c_ref, rows, mask):
    ws = []
    for hh in range(2):
        c = c_ref[hh, rows, :]
        w = jnp.exp2(ys[hh] + c)
        if mask is not None:
            w = jnp.where(mask, w, 0.0)
        if tots is not None:
            c_ref[hh, rows, :] = c + tots[hh]
        ws.append(w.astype(BF16))
    return jnp.concatenate(ws, axis=1)


class _QueryTileSweep:
    def __init__(self, qt, q_ref, o_ref, shared, scratch):
        self.qt, self.q_ref, self.o_ref = qt, q_ref, o_ref
        self.kt_ref, self.v_ref, self.ktm_ref, self.vm_ref, self.kb_ref, self.kbm_ref, self.u2 = shared
        self.qx_ref, self.c_ref, self.acc_ref, self.zn_ref, self.wl_ref, self.ym_ref = scratch
        self.tq = q_ref.shape[0]
        self.nsub = self.tq // KBLK
        self.full = slice(0, self.tq)

    def scores(self, rows, kt, kb):
        zero = jnp.zeros((HEAD_DIM, KBLK), BF16)
        k2 = jnp.concatenate([jnp.concatenate([kt[:HEAD_DIM], zero], axis=0),
                              jnp.concatenate([zero, kt[HEAD_DIM:]], axis=0)], axis=1)
        return jnp.dot(self.qx_ref[rows, :], jnp.concatenate([k2, kb], axis=0), preferred_element_type=F32)

    def add_pv(self, rows, w2, v):
        lane = lax.broadcasted_iota(jnp.int32, v.shape, 1)
        zero = jnp.zeros_like(v)
        v2 = jnp.concatenate([jnp.where(lane < HEAD_DIM, v, zero), jnp.where(lane >= HEAD_DIM, v, zero)], axis=0)
        self.acc_ref[rows, :] += jnp.dot(w2, v2, preferred_element_type=F32)

    def overlapping_blocks(self):
        tq, nsub, qt = self.tq, self.nsub, self.qt
        self.qx_ref[:, :LANES] = self.q_ref[...]
        lane = lax.broadcasted_iota(jnp.int32, (tq, LANES), 1)
        self.qx_ref[:, LANES:] = jnp.where(lane < BIAS_LIMBS, 1.0, 0.0).astype(BF16)
        self.c_ref[...] = jnp.zeros_like(self.c_ref)
        self.acc_ref[...] = jnp.zeros_like(self.acc_ref)
        for jj in reversed(range(nsub)):
            j = qt * nsub + jj
            rows = slice(jj * KBLK, tq)
            row = lax.broadcasted_iota(jnp.int32, (tq - jj * KBLK, KBLK), 0)
            col = lax.broadcasted_iota(jnp.int32, (tq - jj * KBLK, KBLK), 1)
            mask = col < row
            z2 = self.scores(rows, self.kt_ref[j], self.kb_ref[...])
            w2 = _sb_weights(*_sb_logits(z2, self.u2, mask), self.c_ref, rows, mask)
            if jj > 0:
                self.add_pv(rows, w2, self.v_ref[j])
            else:
                self.wl_ref[...] = w2
            yield
        self.zn_ref[...] = self.scores(self.full, self.kt_ref[jnp.maximum(qt * nsub - 1, 0)], self.kb_ref[...])
        ys_meta, _ = _sb_logits(self.scores(self.full, self.ktm_ref[...], self.kbm_ref[...]), self.u2, None)
        self.ym_ref[...] = jnp.concatenate(ys_meta, axis=1)
        yield

    def earlier_blocks(self):
        nsub, qt, full = self.nsub, self.qt, self.full

        def body(i, j_pending):
            self.add_pv(full, self.wl_ref[...], self.v_ref[j_pending])
            j0 = (qt - i) * nsub - 1
            for jj in range(nsub):
                j = j0 - jj
                z2 = self.zn_ref[...] if jj == 0 else self.scores(full, self.kt_ref[j], self.kb_ref[...])
                w2 = _sb_weights(*_sb_logits(z2, self.u2, None), self.c_ref, full, None)
                if jj < nsub - 1:
                    self.add_pv(full, w2, self.v_ref[j])
                else:
                    self.wl_ref[...] = w2
            self.zn_ref[...] = self.scores(full, self.kt_ref[jnp.maximum(j0 - nsub, 0)], self.kb_ref[...])
            return j0 - (nsub - 1)

        self.j_pending = lax.fori_loop(0, qt, body, qt * nsub)

    def meta_block(self):
        self.add_pv(self.full, self.wl_ref[...], self.v_ref[self.j_pending])
        ys_meta = [self.ym_ref[:, hh * KBLK:(hh + 1) * KBLK] for hh in range(2)]
        self.add_pv(self.full, _sb_weights(ys_meta, None, self.c_ref, self.full, None), self.vm_ref[...])
        self.o_ref[...] = self.acc_ref[...].astype(self.o_ref.dtype)


def _attn_kernel(qa_ref, qb_ref, kt_ref, v_ref, ktm_ref, vm_ref, kb_ref, kbm_ref, u2_ref, o_ref, *scratch):
    j = pl.program_id(2)
    n_tiles = o_ref.shape[0]
    shared = (kt_ref, v_ref, ktm_ref, vm_ref, kb_ref, kbm_ref, u2_ref[...])
    tiles = [
        _QueryTileSweep(j, qa_ref, o_ref.at[j], shared, [s.at[0] for s in scratch]),
        _QueryTileSweep(n_tiles - 1 - j, qb_ref, o_ref.at[n_tiles - 1 - j], shared, [s.at[1] for s in scratch]),
    ]
    for _ in zip(*[t.overlapping_blocks() for t in tiles]):
        pass
    for t in tiles:
        t.earlier_blocks()
    for t in tiles:
        t.meta_block()


def _bias_rows(bias2, n_pad):
    limbs = []
    r = bias2
    for _ in range(BIAS_LIMBS):
        limb = r.astype(BF16)
        limbs.append(limb)
        r = r - limb.astype(F32)
    rows = jnp.repeat(jnp.stack(limbs).reshape(BIAS_LIMBS, N_PAIR, 2), KBLK, axis=2)
    key = jnp.arange(2 * KBLK) % KBLK
    pad_row = jnp.where(jnp.arange(BIAS_LIMBS)[:, None, None] == 0, -1e30, 0.0).astype(BF16)
    rows = jnp.where(key[None, None, :] < n_pad, pad_row, rows)
    out = jnp.zeros((N_PAIR, LANES, 2 * KBLK), BF16)
    return out.at[:, :BIAS_LIMBS, :].set(jnp.transpose(rows, (1, 0, 2)))


def _attn_call(bias2, q, kt, v, ktm, vm, u2, tq):
    b, t, _ = q.shape
    nkb = t // KBLK
    n_tiles = t // tq
    assert n_tiles % 2 == 0
    pair_block = lambda rows, cols: pl.BlockSpec((None, rows, cols), lambda i, p, j: (p, 0, 0))
    tile_a = pl.BlockSpec((None, tq, LANES), lambda i, p, j: (i, j, p))
    tile_b = pl.BlockSpec((None, tq, LANES), lambda i, p, j: (i, n_tiles - 1 - j, p))
    return pl.pallas_call(
        _attn_kernel,
        grid=(b, N_PAIR, n_tiles // 2),
        in_specs=[
            tile_a, tile_b,
            pl.BlockSpec((None, nkb, LANES, KBLK), lambda i, p, j: (i, 0, p, 0)),
            pl.BlockSpec((None, nkb, KBLK, LANES), lambda i, p, j: (i, 0, 0, p)),
            pl.BlockSpec((LANES, KBLK), lambda i, p, j: (p, 0)),
            pl.BlockSpec((KBLK, LANES), lambda i, p, j: (0, p)),
            pair_block(LANES, 2 * KBLK),
            pair_block(LANES, 2 * KBLK),
            pl.BlockSpec((2 * KBLK, 2 * KBLK), lambda i, p, j: (0, 0)),
        ],
        out_specs=pl.BlockSpec((None, n_tiles, tq, LANES), lambda i, p, j: (i, 0, 0, p)),
        out_shape=jax.ShapeDtypeStruct((b, n_tiles, tq, D_ATT), BF16),
        scratch_shapes=[pltpu.VMEM((2, tq, 2 * LANES), BF16), pltpu.VMEM((2, 2, tq, KBLK), F32),
                        pltpu.VMEM((2, tq, LANES), F32), pltpu.VMEM((2, tq, 2 * KBLK), F32),
                        pltpu.VMEM((2, tq, 2 * KBLK), BF16), pltpu.VMEM((2, tq, 2 * KBLK), F32)],
        compiler_params=pltpu.CompilerParams(
            dimension_semantics=("arbitrary", "arbitrary", "arbitrary"), vmem_limit_bytes=VMEM_LIMIT),
        name="sb_attn",
    )(q, q, kt, v.reshape(b, nkb, KBLK, D_ATT), ktm, vm,
      _bias_rows(bias2, 0), _bias_rows(bias2, KBLK - N_META), u2)


def _ffn_steps(x_ref, o_ref, yc_ref, gmix_ref, wg_ref, wa_ref, wc_ref, wo_ref,
               gffn_ref, wgate_ref, wup_ref, wdown_ref, gfin_ref, y_ref, ff_chunks):
    s = {}

    def gates():
        s["h"] = _rms(x_ref[...], gmix_ref[...]).astype(BF16)
        ga = jnp.dot(s["h"], wg_ref[:, :D_MODEL], preferred_element_type=F32)
        ya = jnp.dot(o_ref[...], wa_ref[...], preferred_element_type=F32)
        s["mrg"] = _sigmoid(ga) * ya

    def merge():
        gb = jnp.dot(s.pop("h"), wg_ref[:, D_MODEL:], preferred_element_type=F32)
        yb = jnp.dot(yc_ref[...], wc_ref[...], preferred_element_type=F32)
        mrg = s.pop("mrg") + _sigmoid(gb) * yb
        x1 = x_ref[...] + jnp.dot(mrg.astype(BF16), wo_ref[...], preferred_element_type=F32)
        s["h2"] = _rms(x1, gffn_ref[...]).astype(BF16)
        s["acc"] = x1

    def gate(c0, c1):
        s["a"] = jnp.dot(s["h2"], wgate_ref[:, c0:c1], preferred_element_type=F32)

    def up(c0, c1):
        a = s.pop("a")
        b = jnp.dot(s["h2"], wup_ref[:, c0:c1], preferred_element_type=F32)
        s["act"] = (a * _sigmoid(a) * b).astype(BF16)

    def down(c0, c1):
        s["acc"] = s["acc"] + jnp.dot(s.pop("act"), wdown_ref[c0:c1, :], preferred_element_type=F32)

    def out():
        y_ref[...] = _rms(s["acc"], gfin_ref[...])

    steps = [gates, merge]
    for c0, c1 in ff_chunks:
        steps += [functools.partial(f, c0, c1) for f in (gate, up, down)]
    return steps + [out]


def _merge_ffn_kernel(*refs, ff_chunks):
    for step in _ffn_steps(*refs, ff_chunks):
        step()


def _suffix_sums_lanes(x):
    lane = lax.broadcasted_iota(jnp.int32, x.shape, 1)
    y = x
    k = 1
    while k < LANES:
        shifted = pltpu.roll(y, LANES - k, axis=1)
        y = y + jnp.where(lane < LANES - k, shifted, 0.0)
        k *= 2
    return y


def _decode_pages(k_pages, v_pages, qb_ref, bias_ref, c_ref, acc_ref, w_ref):
    n = k_pages.shape[0]
    zs = []
    for i in range(n):
        rows = []
        for h in range(SB_HEADS):
            hs = slice(h * HEAD_DIM, (h + 1) * HEAD_DIM)
            rows.append(jnp.sum(k_pages[i, hs, :] * qb_ref[hs, :], axis=0, keepdims=True))
        zs.append(jnp.concatenate(rows, axis=0) + bias_ref[...])
    z = jnp.concatenate(zs, axis=0)
    lb, l1m = _log_sigmoid_pair(z)
    later = _suffix_sums_lanes(l1m) - l1m
    tot = jnp.sum(l1m, axis=1, keepdims=True)
    c = c_ref[...]
    for i in range(n):
        rows = slice(i * SB_HEADS, (i + 1) * SB_HEADS)
        w_ref[rows, :] = jnp.exp(lb[rows] + later[rows] + c)
        c = c + tot[rows]
    c_ref[...] = c
    for h in range(SB_HEADS):
        hs = slice(h * HEAD_DIM, (h + 1) * HEAD_DIM)
        a = acc_ref[hs, :]
        for i in range(n):
            r = i * SB_HEADS + h
            a = a + v_pages[i, hs, :] * w_ref[r:r + 1, :]
        acc_ref[hs, :] = a


def _lane_sums_row(acc, ones):
    a0 = acc.astype(BF16)
    r1 = acc - a0.astype(F32)
    a1 = r1.astype(BF16)
    a2 = (r1 - a1.astype(F32)).astype(BF16)
    tot = (lax.dot_general(ones, a0, _NT, preferred_element_type=F32)
           + lax.dot_general(ones, a1, _NT, preferred_element_type=F32)
           + lax.dot_general(ones, a2, _NT, preferred_element_type=F32))
    return tot[0:1]


def _merge_ffn_decode_kernel(pt_ref, *refs, ff_chunks, n_pages):
    ffn_refs, (qb_ref, bias_ref, ones_ref, kc_hbm, vc_hbm) = refs[:13], refs[13:18]
    y_ref, os_ref, kbuf, vbuf, sem, c_ref, acc_ref, w_ref = refs[18:]
    seq = pl.program_id(0)
    n_seq = pl.num_programs(0)
    n_chunks = n_pages // DEC_CHUNK
    ahead = DEC_SLOTS - 1

    def copies(g):
        sq = jnp.minimum(g // n_chunks, n_seq - 1)
        chunk = g % n_chunks
        slot = g % DEC_SLOTS
        out = []
        for i in range(DEC_CHUNK):
            page = pt_ref[sq * n_pages + (n_pages - 1 - (chunk * DEC_CHUNK + i))]
            out.append(pltpu.make_async_copy(kc_hbm.at[page], kbuf.at[slot, i], sem.at[0, slot]))
            out.append(pltpu.make_async_copy(vc_hbm.at[page], vbuf.at[slot, i], sem.at[1, slot]))
        return out

    g0 = seq * n_chunks

    @pl.when(seq == 0)
    def _():
        for g in range(ahead):
            for cp in copies(g):
                cp.start()

    c_ref[...] = jnp.zeros_like(c_ref)
    acc_ref[...] = jnp.zeros_like(acc_ref)
    steps = _ffn_steps(*ffn_refs, y_ref, ff_chunks)
    assert len(steps) == n_chunks + 1
    for chunk in range(n_chunks):
        g = g0 + chunk
        for cp in copies(g + ahead):
            cp.start()
        for cp in copies(g):
            cp.wait()
        slot = g % DEC_SLOTS
        _decode_pages(kbuf.at[slot], vbuf.at[slot], qb_ref, bias_ref, c_ref, acc_ref, w_ref)
        steps[chunk]()
    steps[n_chunks]()
    os_ref[0] = _lane_sums_row(acc_ref[...], ones_ref[...]).astype(os_ref.dtype)

    @pl.when(seq == n_seq - 1)
    def _():
        for k in range(ahead):
            for cp in copies(n_seq * n_chunks + k):
                cp.wait()


def _ff_chunks(d_ff, n=2):
    tiles = pl.cdiv(d_ff, MXU_DIM)
    cuts = [min(pl.cdiv(tiles * i, n) * MXU_DIM, d_ff) for i in range(n + 1)]
    return tuple(zip(cuts[:-1], cuts[1:]))


def _merge_ffn_call(x, o, yc, gmix, wg, wa, wc, wo, gffn, wgate, wup, wdown, gfin, tm):
    m = x.shape[0]
    d_ff = wgate.shape[1]
    row_spec = lambda width: pl.BlockSpec((tm, width), lambda i: (i, 0))
    return pl.pallas_call(
        functools.partial(_merge_ffn_kernel, ff_chunks=_ff_chunks(d_ff)),
        grid=(m // tm,),
        in_specs=[
            row_spec(D_MODEL), row_spec(D_ATT), row_spec(D_CONV),
            _const_spec((1, D_MODEL)),
            _const_spec(wg.shape), _const_spec(wa.shape), _const_spec(wc.shape), _const_spec(wo.shape),
            _const_spec((1, D_MODEL)),
            _const_spec(wgate.shape), _const_spec(wup.shape), _const_spec(wdown.shape),
            _const_spec((1, D_MODEL)),
        ],
        out_specs=row_spec(D_MODEL),
        out_shape=jax.ShapeDtypeStruct((m, D_MODEL), F32),
        compiler_params=pltpu.CompilerParams(
            dimension_semantics=("arbitrary",), vmem_limit_bytes=VMEM_LIMIT),
        name="merge_ffn",
    )(x, o, yc, gmix, wg, wa, wc, wo, gffn, wgate, wup, wdown, gfin)


def _merge_ffn_decode_call(page_table, ffn_args, qb, bias_b, ones, cache_kt, cache_vt, tm):
    x, wgate = ffn_args[0], ffn_args[9]
    m = x.shape[0]
    n_dec, n_pages = page_table.shape
    assert m // tm == n_dec
    row_spec = lambda width: pl.BlockSpec((tm, width), lambda i, pt: (i, 0))
    const = lambda a: _const_spec(a.shape)
    any_spec = pl.BlockSpec(memory_space=pl.ANY)
    grid_spec = pltpu.PrefetchScalarGridSpec(
        num_scalar_prefetch=1,
        grid=(n_dec,),
        in_specs=[row_spec(D_MODEL), row_spec(D_ATT), row_spec(D_CONV)] + [const(a) for a in ffn_args[3:]] + [
            pl.BlockSpec((None, D_ATT, LANES), lambda i, pt: (i, 0, 0)),
            const(bias_b), const(ones), any_spec, any_spec,
        ],
        out_specs=(row_spec(D_MODEL), pl.BlockSpec((1, 1, D_ATT), lambda i, pt: (i, 0, 0))),
        scratch_shapes=[
            pltpu.VMEM((DEC_SLOTS, DEC_CHUNK, D_ATT, PAGE), F32),
            pltpu.VMEM((DEC_SLOTS, DEC_CHUNK, D_ATT, PAGE), F32),
            pltpu.SemaphoreType.DMA((2, DEC_SLOTS)),
            pltpu.VMEM((SB_HEADS, LANES), F32),
            pltpu.VMEM((D_ATT, PAGE), F32),
            pltpu.VMEM((DEC_CHUNK * SB_HEADS, PAGE), F32),
        ],
    )
    return pl.pallas_call(
        functools.partial(_merge_ffn_decode_kernel, ff_chunks=_ff_chunks(wgate.shape[1]), n_pages=n_pages),
        grid_spec=grid_spec,
        out_shape=(jax.ShapeDtypeStruct((m, D_MODEL), F32), jax.ShapeDtypeStruct((n_dec, 1, D_ATT), BF16)),
        compiler_params=pltpu.CompilerParams(
            dimension_semantics=("arbitrary",), vmem_limit_bytes=VMEM_LIMIT),
        name="merge_ffn_decode",
    )(page_table.reshape(-1), *ffn_args, qb, bias_b, ones, cache_kt, cache_vt)


def _sample_proj_kernel(x_ref, g_ref, w_ref, cw_ref, s0_ref, s1_ref, q_ref, kf_ref, vf_ref, yc_ref, u_ref):
    h = _rms(x_ref[...], g_ref[...]).astype(BF16)
    p = jnp.dot(h, w_ref[...], preferred_element_type=F32)
    q_ref[...] = p[:, 0:D_ATT] * (HEAD_DIM ** -0.5)
    kf_ref[...] = p[:, D_ATT:2 * D_ATT]
    vf_ref[...] = p[:, 2 * D_ATT:3 * D_ATT]
    bg = p[:, 3 * D_ATT:3 * D_ATT + D_CONV]
    u = p[:, 3 * D_ATT + D_CONV:3 * D_ATT + 2 * D_CONV] * p[:, 3 * D_ATT + 2 * D_CONV:3 * D_ATT + 3 * D_CONV]
    cw = cw_ref[...]
    yc_ref[...] = (bg * (cw[0:1] * s0_ref[...] + cw[1:2] * s1_ref[...] + cw[2:3] * u)).astype(BF16)
    u_ref[...] = u


def _sample_proj_call(x, g, w, cw, s0, s1):
    n = x.shape[0]
    full = lambda shape: pl.BlockSpec(shape, lambda i: (0,) * len(shape))
    outs = (
        jax.ShapeDtypeStruct((n, D_ATT), F32),
        jax.ShapeDtypeStruct((n, D_ATT), F32),
        jax.ShapeDtypeStruct((n, D_ATT), F32),
        jax.ShapeDtypeStruct((n, D_CONV), BF16),
        jax.ShapeDtypeStruct((n, D_CONV), F32),
    )
    return pl.pallas_call(
        _sample_proj_kernel,
        grid=(1,),
        in_specs=[full(x.shape), full(g.shape), full(w.shape), full(cw.shape), full(s0.shape), full(s1.shape)],
        out_specs=tuple(full(o.shape) for o in outs),
        out_shape=outs,
        compiler_params=pltpu.CompilerParams(vmem_limit_bytes=VMEM_LIMIT),
        name="sample_proj",
    )(x, g, w, cw, s0, s1)


def _suffix_cols(inclusive, sign, dtype):
    j = np.arange(KBLK)[:, None]
    s = np.arange(KBLK)[None, :]
    later = (j >= s) if inclusive else (j > s)
    blk = sign * np.concatenate([later.astype(np.float32), np.ones((KBLK, KBLK), np.float32)], axis=1)
    return jnp.asarray(np.concatenate([blk, blk], axis=0), dtype=dtype)


def _pages_as_rows(cache):
    n_phys = cache.shape[0]
    return jnp.transpose(cache, (0, 2, 3, 1)).reshape(n_phys, D_ATT, PAGE)


def _state_from_rows(rows_t):
    b, _, n_pos = rows_t.shape
    return jnp.transpose(rows_t.reshape(b, SB_HEADS, HEAD_DIM, n_pos), (0, 3, 1, 2))[None]


def kernel(x_prompt, x_sample, cache_k, cache_v, state_conv, page_table, meta_tokens, norm_mix, w_in,
           sb_bias, conv_w, w_att_out, w_conv_out, w_o, norm_ffn, w_gate, w_up, w_down, norm_final):
    assert w_in.shape[0] == 1, "single-layer step: the meta rows' block output never reaches a returned array"
    n_prompt, seq, _ = x_prompt.shape
    n_dec, t_dec, _ = x_sample.shape
    assert t_dec == 1

    w_in_b = w_in[0].astype(BF16)
    n_qkvc = 3 * D_ATT + 3 * D_CONV
    w_qkvc = w_in_b[:, :n_qkvc]
    w_gates = w_in_b[:, n_qkvc:]
    gmix = norm_mix[0][None, :]
    gffn = norm_ffn[0][None, :]
    gfin = norm_final[None, :]
    cw = conv_w[0]
    bias = sb_bias[0]
    ffn_w = (w_att_out[0].astype(BF16), w_conv_out[0].astype(BF16), w_o[0].astype(BF16), gffn,
             w_gate[0].astype(BF16), w_up[0].astype(BF16), w_down[0].astype(BF16), gfin)

    zero_halo = jnp.zeros((CONV_K - 1, D_CONV), F32)
    x_meta = jnp.pad(meta_tokens, ((KBLK - N_META, 0), (0, 0)))[None]
    no_front = jnp.zeros((D_ATT, LANES), F32)
    _, ktm_b, vm_b, ktm_f, vtm_f, _, u_meta = _proj_call(
        x_meta, gmix, w_qkvc, cw, zero_halo, no_front, no_front, 0, KBLK)

    q_b, kt_b, v_b, kt_f, vt_f, yc, conv_p = _proj_call(
        x_prompt, gmix, w_qkvc, cw, u_meta[0], ktm_f[0], vtm_f[0], N_META, 512)
    o_att = _attn_call(bias * LOG2E, q_b, kt_b, v_b, ktm_b[0, 0], vm_b[0],
                       _suffix_cols(inclusive=True, sign=-1.0, dtype=F32), tq=1024)

    st = state_conv[0]
    q_s, k_s, v_s, yc_s, u_s = _sample_proj_call(x_sample[:, 0], gmix, w_qkvc, cw, st[:, 0], st[:, 1])
    qb = jnp.broadcast_to(q_s[:, :, None], (n_dec, D_ATT, LANES))
    bias_b = jnp.broadcast_to(bias[:, None], (SB_HEADS, LANES))
    m_rows = n_prompt * seq
    prompt_rows = (x_prompt.reshape(m_rows, D_MODEL), o_att.reshape(m_rows, D_ATT), yc.reshape(m_rows, D_CONV))
    y_p, o_s = _merge_ffn_decode_call(
        page_table, prompt_rows + (gmix, w_gates) + ffn_w, qb, bias_b, jnp.ones((8, LANES), BF16),
        _pages_as_rows(cache_k[0]), _pages_as_rows(cache_v[0]), tm=m_rows // n_dec)
    y_s = _merge_ffn_call(x_sample[:, 0], o_s[:, 0], yc_s, gmix, w_gates, *ffn_w, tm=n_dec)

    return (
        y_p.reshape(n_prompt, seq, D_MODEL),
        y_s.reshape(n_dec, 1, D_MODEL),
        _state_from_rows(kt_f),
        _state_from_rows(vt_f),
        conv_p[None],
        k_s.reshape(1, n_dec, 1, SB_HEADS, HEAD_DIM),
        v_s.reshape(1, n_dec, 1, SB_HEADS, HEAD_DIM),
        jnp.stack([st[:, 1], u_s], axis=1)[None],
    )
```

```python
import functools

import jax
import jax.numpy as jnp
import numpy as np
from jax import lax
from jax.experimental import pallas as pl
from jax.experimental.pallas import tpu as pltpu

F32 = jnp.float32
BF16 = jnp.bfloat16

D_MODEL = 1024
N_META = 16
SB_HEADS = 8
HEAD_DIM = 64
D_ATT = SB_HEADS * HEAD_DIM
D_CONV = 512
CONV_K = 3
EPS = 1e-6
PAGE = 128

LANES = 128
MXU_DIM = 256
KBLK = 128
N_PAIR = D_ATT // LANES
BIAS_LIMBS = 3
PROJ_ROWS = 512
ATTN_ROWS = 1024
DEC_CHUNK = 8
DEC_SLOTS = 3
VMEM_LIMIT = 60 * 1024 * 1024
LOG2E = 1.4426950408889634

_NT = (((1,), (1,)), ((), ()))


def _rms(x, g):
    return x * lax.rsqrt(jnp.mean(x * x, axis=-1, keepdims=True) + EPS) * g


def _sigmoid(x):
    return 1.0 / (1.0 + jnp.exp(-x))


def _log_sigmoid_pair(z):
    l = jnp.log(1.0 + jnp.exp(-jnp.abs(z)))
    lb = jnp.minimum(z, 0.0) - l
    return lb, lb - z


def _const_spec(shape):
    nd = len(shape)
    return pl.BlockSpec(shape, lambda *_: (0,) * nd, pipeline_mode=pl.Buffered(1))


def _proj_kernel(x_ref, g_ref, w_ref, cw_ref, halo_ref, fk_ref, fv_ref,
                 q_ref, ktb_ref, vb_ref, ktf_ref, vtf_ref, yc_ref, st_ref,
                 carry_ref, kcar_ref, vcar_ref, *, n_front):
    m = pl.program_id(1)
    tm = x_ref.shape[1]

    def put_state(ref, car_ref, rows):
        if n_front == 0:
            ref[0] = rows.T
            return
        ref[0] = jnp.concatenate([car_ref[...], rows[:tm - n_front]], axis=0).T
        car_ref[...] = rows[tm - n_front:]

    def tile():
        h = _rms(x_ref[0], g_ref[...]).astype(BF16)
        p = jnp.dot(h, w_ref[...], preferred_element_type=F32)
        q_ref[0] = (p[:, 0:D_ATT] * (HEAD_DIM ** -0.5 * LOG2E)).astype(BF16)
        k = p[:, D_ATT:2 * D_ATT]
        v = p[:, 2 * D_ATT:3 * D_ATT]
        put_state(ktf_ref, kcar_ref, k)
        put_state(vtf_ref, vcar_ref, v)
        ktb = k.T.astype(BF16)
        for c in range(tm // KBLK):
            ktb_ref[0, c] = ktb[:, c * KBLK:(c + 1) * KBLK]
        vb_ref[0] = v.astype(BF16)

        bg = p[:, 3 * D_ATT:3 * D_ATT + D_CONV]
        u = p[:, 3 * D_ATT + D_CONV:3 * D_ATT + 2 * D_CONV] * p[:, 3 * D_ATT + 2 * D_CONV:3 * D_ATT + 3 * D_CONV]
        prev2 = carry_ref[0:1, :]
        prev1 = carry_ref[1:2, :]
        row = lax.broadcasted_iota(jnp.int32, u.shape, 0)
        u1 = jnp.where(row == 0, prev1, pltpu.roll(u, 1, 0))
        u2 = jnp.where(row == 0, prev2, jnp.where(row == 1, prev1, pltpu.roll(u, 2, 0)))
        cw = cw_ref[...]
        yc = bg * (cw[0:1] * u2 + cw[1:2] * u1 + cw[2:3] * u)
        yc_ref[0] = yc.astype(BF16)
        tail = u[tm - 2:tm, :]
        carry_ref[0:2, :] = tail
        st_ref[0] = tail

    @pl.when(m == 0)
    def _():
        carry_ref[0:2, :] = halo_ref[...]
        if n_front:
            kcar_ref[...] = fk_ref[...].T[LANES - n_front:, :]
            vcar_ref[...] = fv_ref[...].T[LANES - n_front:, :]

    if n_front == 0:
        tile()
    else:
        n_tiles = pl.num_programs(1) - 1
        pl.when(m < n_tiles)(tile)

        @pl.when(m == n_tiles)
        def _():
            pad = jnp.zeros((LANES - n_front, D_ATT), F32)
            ktf_ref[0, :, :LANES] = jnp.concatenate([kcar_ref[...], pad], axis=0).T
            vtf_ref[0, :, :LANES] = jnp.concatenate([vcar_ref[...], pad], axis=0).T


def _proj_call(x, g, w, cw, halo, front_k, front_v, n_front, tm):
    b, t, _ = x.shape
    assert tm % KBLK == 0
    n_tiles = t // tm
    tile_idx = (lambda j: jnp.minimum(j, n_tiles - 1)) if n_front else (lambda j: j)
    row_spec = lambda width: pl.BlockSpec((1, tm, width), lambda i, j: (i, tile_idx(j), 0))
    col_spec = pl.BlockSpec((1, D_ATT, tm), lambda i, j: (i, 0, j))
    outs = (
        jax.ShapeDtypeStruct((b, t, D_ATT), BF16),
        jax.ShapeDtypeStruct((b, t // KBLK, D_ATT, KBLK), BF16),
        jax.ShapeDtypeStruct((b, t, D_ATT), BF16),
        jax.ShapeDtypeStruct((b, D_ATT, n_front + t), F32),
        jax.ShapeDtypeStruct((b, D_ATT, n_front + t), F32),
        jax.ShapeDtypeStruct((b, t, D_CONV), BF16),
        jax.ShapeDtypeStruct((b, CONV_K - 1, D_CONV), F32),
    )
    return pl.pallas_call(
        functools.partial(_proj_kernel, n_front=n_front),
        grid=(b, n_tiles + (1 if n_front else 0)),
        in_specs=[
            row_spec(D_MODEL),
            _const_spec((1, D_MODEL)),
            _const_spec(w.shape),
            _const_spec((CONV_K, D_CONV)),
            _const_spec((CONV_K - 1, D_CONV)),
            _const_spec((D_ATT, LANES)),
            _const_spec((D_ATT, LANES)),
        ],
        out_specs=(
            row_spec(D_ATT),
            pl.BlockSpec((1, tm // KBLK, D_ATT, KBLK), lambda i, j: (i, tile_idx(j), 0, 0)),
            row_spec(D_ATT),
            col_spec, col_spec,
            row_spec(D_CONV),
            pl.BlockSpec((1, CONV_K - 1, D_CONV), lambda i, j: (i, 0, 0)),
        ),
        out_shape=outs,
        scratch_shapes=[pltpu.VMEM((8, D_CONV), F32), pltpu.VMEM((max(n_front, 8), D_ATT), F32),
                        pltpu.VMEM((max(n_front, 8), D_ATT), F32)],
        compiler_params=pltpu.CompilerParams(
            dimension_semantics=("arbitrary", "arbitrary"), vmem_limit_bytes=VMEM_LIMIT),
        name="proj_conv",
    )(x, g, w, cw, halo, front_k, front_v)


def _neg_log2_one_minus_beta_split(z, mask=None):
    neg_abs = pltpu.bitcast(pltpu.bitcast(z, jnp.uint32) | jnp.uint32(0x80000000), F32)
    p = jnp.maximum(z, 0.0) + jnp.log(1.0 + jnp.exp2(neg_abs)) * LOG2E
    if mask is not None:
        p = jnp.where(mask, p, 0.0)
    hi = pltpu.bitcast(pltpu.bitcast(p, jnp.uint32) & jnp.uint32(0xFFFF0000), F32)
    return hi, p - hi


def _sb_logits(z2, u2, mask):
    ys, tots = [], []
    for hh in range(2):
        z = z2[:, hh * KBLK:(hh + 1) * KBLK]
        hi, lo = _neg_log2_one_minus_beta_split(z, mask)
        s_ext = jnp.dot(jnp.concatenate([hi, lo], axis=1), u2, preferred_element_type=F32)
        ys.append(z + s_ext[:, :KBLK])
        tots.append(s_ext[:, KBLK:])
    return ys, tots


def _sb_weights(ys, tots, c_ref, rows, mask):
    ws = []
    for hh in range(2):
        c = c_ref[hh, rows, :]
        w = jnp.exp2(ys[hh] + c)
        if mask is not None:
            w = jnp.where(mask, w, 0.0)
        if tots is not None:
            c_ref[hh, rows, :] = c + tots[hh]
        ws.append(w.astype(BF16))
    return jnp.concatenate(ws, axis=1)


class _QueryTileSweep:
    def __init__(self, qt, q_ref, o_ref, shared, scratch):
        self.qt, self.q_ref, self.o_ref = qt, q_ref, o_ref
        self.kt_ref, self.v_ref, self.kmp_ref, self.vmp_ref, self.kb_ref, self.u2, self.um_ref = shared
        self.qx_ref, self.c_ref, self.acc_ref, self.zn_ref, self.wl_ref, self.ym_ref = scratch
        self.tq = q_ref.shape[0]
        self.nsub = self.tq // KBLK
        self.full = slice(0, self.tq)

    def scores(self, rows, kt, kb):
        zero = jnp.zeros((HEAD_DIM, KBLK), BF16)
        k2 = jnp.concatenate([jnp.concatenate([kt[:HEAD_DIM], zero], axis=0),
                              jnp.concatenate([zero, kt[HEAD_DIM:]], axis=0)], axis=1)
        return jnp.dot(self.qx_ref[rows, :], jnp.concatenate([k2, kb], axis=0), preferred_element_type=F32)

    def add_pv(self, rows, w2, v):
        lane = lax.broadcasted_iota(jnp.int32, v.shape, 1)
        zero = jnp.zeros_like(v)
        v2 = jnp.concatenate([jnp.where(lane < HEAD_DIM, v, zero), jnp.where(lane >= HEAD_DIM, v, zero)], axis=0)
        self.acc_ref[rows, :] += jnp.dot(w2, v2, preferred_element_type=F32)

    def overlapping_blocks(self):
        tq, nsub, qt = self.tq, self.nsub, self.qt
        self.qx_ref[:, :LANES] = self.q_ref[...]
        lane = lax.broadcasted_iota(jnp.int32, (tq, LANES), 1)
        self.qx_ref[:, LANES:] = jnp.where(lane < BIAS_LIMBS, 1.0, 0.0).astype(BF16)
        self.c_ref[...] = jnp.zeros_like(self.c_ref)
        self.acc_ref[...] = jnp.zeros_like(self.acc_ref)
        for jj in reversed(range(nsub)):
            j = qt * nsub + jj
            rows = slice(jj * KBLK, tq)
            row = lax.broadcasted_iota(jnp.int32, (tq - jj * KBLK, KBLK), 0)
            col = lax.broadcasted_iota(jnp.int32, (tq - jj * KBLK, KBLK), 1)
            mask = col < row
            z2 = self.scores(rows, self.kt_ref[j], self.kb_ref[...])
            w2 = _sb_weights(*_sb_logits(z2, self.u2, mask), self.c_ref, rows, mask)
            if jj > 0:
                self.add_pv(rows, w2, self.v_ref[j])
            else:
                self.wl_ref[...] = w2
            yield
        self.zn_ref[...] = self.scores(self.full, self.kt_ref[jnp.maximum(qt * nsub - 1, 0)], self.kb_ref[...])
        z = jnp.dot(self.qx_ref[...], self.kmp_ref[...], preferred_element_type=F32)
        hi, lo = _neg_log2_one_minus_beta_split(z)
        self.ym_ref[...] = z + jnp.dot(jnp.concatenate([hi, lo], axis=1), self.um_ref[...],
                                       preferred_element_type=F32)
        yield

    def earlier_blocks(self):
        nsub, qt, full = self.nsub, self.qt, self.full

        def body(i, j_pending):
            self.add_pv(full, self.wl_ref[...], self.v_ref[j_pending])
            j0 = (qt - i) * nsub - 1
            for jj in range(nsub):
                j = j0 - jj
                z2 = self.zn_ref[...] if jj == 0 else self.scores(full, self.kt_ref[j], self.kb_ref[...])
                w2 = _sb_weights(*_sb_logits(z2, self.u2, None), self.c_ref, full, None)
                if jj < nsub - 1:
                    self.add_pv(full, w2, self.v_ref[j])
                else:
                    self.wl_ref[...] = w2
            self.zn_ref[...] = self.scores(full, self.kt_ref[jnp.maximum(j0 - nsub, 0)], self.kb_ref[...])
            return j0 - (nsub - 1)

        self.j_pending = lax.fori_loop(0, qt, body, qt * nsub)

    def meta_block(self):
        self.add_pv(self.full, self.wl_ref[...], self.v_ref[self.j_pending])
        lane = lax.broadcasted_iota(jnp.int32, (self.tq, LANES), 1)
        c = jnp.where(lane < N_META, self.c_ref[0], self.c_ref[1])
        w = jnp.exp2(self.ym_ref[...] + c).astype(BF16)
        self.acc_ref[...] += jnp.dot(w, self.vmp_ref[...], preferred_element_type=F32)
        self.o_ref[...] = self.acc_ref[...].astype(self.o_ref.dtype)


def _attn_kernel(qa_ref, qb_ref, kt_ref, v_ref, kmp_ref, vmp_ref, kb_ref, u2_ref, um_ref, o_ref, *scratch):
    j = pl.program_id(2)
    n_tiles = o_ref.shape[0]
    shared = (kt_ref, v_ref, kmp_ref, vmp_ref, kb_ref, u2_ref[...], um_ref)
    tiles = [
        _QueryTileSweep(j, qa_ref, o_ref.at[j], shared, [s.at[0] for s in scratch]),
        _QueryTileSweep(n_tiles - 1 - j, qb_ref, o_ref.at[n_tiles - 1 - j], shared, [s.at[1] for s in scratch]),
    ]
    for _ in zip(*[t.overlapping_blocks() for t in tiles]):
        pass
    for t in tiles:
        t.earlier_blocks()
    for t in tiles:
        t.meta_block()


def _bias_limbs(bias2):
    limbs = []
    r = bias2
    for _ in range(BIAS_LIMBS):
        limb = r.astype(BF16)
        limbs.append(limb)
        r = r - limb.astype(F32)
    return jnp.stack(limbs).reshape(BIAS_LIMBS, N_PAIR, 2)


def _bias_rows(bias2):
    rows = jnp.repeat(_bias_limbs(bias2), KBLK, axis=2)
    out = jnp.zeros((N_PAIR, LANES, 2 * KBLK), BF16)
    return out.at[:, :BIAS_LIMBS, :].set(jnp.transpose(rows, (1, 0, 2)))


def _meta_operands(bias2, ktm, vm):
    kt_meta = ktm[:, KBLK - N_META:].reshape(N_PAIR, 2, HEAD_DIM, N_META)
    v_meta = vm[KBLK - N_META:, :].reshape(N_META, N_PAIR, 2, HEAD_DIM)
    limbs = _bias_limbs(bias2)
    keys = jnp.zeros((N_PAIR, 2 * LANES, KBLK), BF16)
    keys = keys.at[:, LANES, 2 * N_META:].set(-1e30)
    values = jnp.zeros((N_PAIR, KBLK, LANES), BF16)
    for hh in range(2):
        lanes = slice(hh * N_META, (hh + 1) * N_META)
        dims = slice(hh * HEAD_DIM, (hh + 1) * HEAD_DIM)
        keys = keys.at[:, dims, lanes].set(kt_meta[:, hh])
        keys = keys.at[:, LANES:LANES + BIAS_LIMBS, lanes].set(
            jnp.broadcast_to(jnp.transpose(limbs[:, :, hh])[:, :, None], (N_PAIR, BIAS_LIMBS, N_META)))
        values = values.at[:, lanes, dims].set(jnp.transpose(v_meta[:, :, hh], (1, 0, 2)))
    j = np.arange(KBLK)[:, None]
    s = np.arange(KBLK)[None, :]
    later = (j >= s) & (j // N_META == s // N_META) & (j < 2 * N_META)
    suffix = jnp.asarray(-np.concatenate([later, later], axis=0).astype(np.float32))
    return keys, values, suffix


def _attn_call(bias2, q, kt, v, ktm, vm, u2, tq):
    b, t, _ = q.shape
    nkb = t // KBLK
    n_tiles = t // tq
    assert n_tiles % 2 == 0
    meta_keys, meta_values, meta_suffix = _meta_operands(bias2, ktm, vm)
    pair_block = lambda rows, cols: pl.BlockSpec((None, rows, cols), lambda i, p, j: (p, 0, 0))
    tile_a = pl.BlockSpec((None, tq, LANES), lambda i, p, j: (i, j, p))
    tile_b = pl.BlockSpec((None, tq, LANES), lambda i, p, j: (i, n_tiles - 1 - j, p))
    return pl.pallas_call(
        _attn_kernel,
        grid=(b, N_PAIR, n_tiles // 2),
        in_specs=[
            tile_a, tile_b,
            pl.BlockSpec((None, nkb, LANES, KBLK), lambda i, p, j: (i, 0, p, 0)),
            pl.BlockSpec((None, nkb, KBLK, LANES), lambda i, p, j: (i, 0, 0, p)),
            pair_block(2 * LANES, KBLK),
            pair_block(KBLK, LANES),
            pair_block(LANES, 2 * KBLK),
            pl.BlockSpec((2 * KBLK, 2 * KBLK), lambda i, p, j: (0, 0)),
            pl.BlockSpec((2 * KBLK, KBLK), lambda i, p, j: (0, 0)),
        ],
        out_specs=pl.BlockSpec((None, n_tiles, tq, LANES), lambda i, p, j: (i, 0, 0, p)),
        out_shape=jax.ShapeDtypeStruct((b, n_tiles, tq, D_ATT), BF16),
        scratch_shapes=[pltpu.VMEM((2, tq, 2 * LANES), BF16), pltpu.VMEM((2, 2, tq, KBLK), F32),
                        pltpu.VMEM((2, tq, LANES), F32), pltpu.VMEM((2, tq, 2 * KBLK), F32),
                        pltpu.VMEM((2, tq, 2 * KBLK), BF16), pltpu.VMEM((2, tq, KBLK), F32)],
        compiler_params=pltpu.CompilerParams(
            dimension_semantics=("arbitrary", "arbitrary", "arbitrary"), vmem_limit_bytes=VMEM_LIMIT),
        name="sb_attn",
    )(q, q, kt, v.reshape(b, nkb, KBLK, D_ATT), meta_keys, meta_values, _bias_rows(bias2), u2, meta_suffix)


def _ffn_steps(x_ref, o_ref, yc_ref, gmix_ref, wg_ref, wa_ref, wc_ref, wo_ref,
               gffn_ref, wgate_ref, wup_ref, wdown_ref, gfin_ref, y_ref, ff_chunks):
    s = {}

    def gates():
        s["h"] = _rms(x_ref[...], gmix_ref[...]).astype(BF16)
        ga = jnp.dot(s["h"], wg_ref[:, :D_MODEL], preferred_element_type=F32)
        ya = jnp.dot(o_ref[...], wa_ref[...], preferred_element_type=F32)
        s["mrg"] = _sigmoid(ga) * ya

    def merge():
        gb = jnp.dot(s.pop("h"), wg_ref[:, D_MODEL:], preferred_element_type=F32)
        yb = jnp.dot(yc_ref[...], wc_ref[...], preferred_element_type=F32)
        mrg = s.pop("mrg") + _sigmoid(gb) * yb
        x1 = x_ref[...] + jnp.dot(mrg.astype(BF16), wo_ref[...], preferred_element_type=F32)
        s["h2"] = _rms(x1, gffn_ref[...]).astype(BF16)
        s["acc"] = x1

    def gate(c0, c1):
        s["a"] = jnp.dot(s["h2"], wgate_ref[:, c0:c1], preferred_element_type=F32)

    def up(c0, c1):
        a = s.pop("a")
        b = jnp.dot(s["h2"], wup_ref[:, c0:c1], preferred_element_type=F32)
        s["act"] = (a * _sigmoid(a) * b).astype(BF16)

    def down(c0, c1):
        s["acc"] = s["acc"] + jnp.dot(s.pop("act"), wdown_ref[c0:c1, :], preferred_element_type=F32)

    def out():
        y_ref[...] = _rms(s["acc"], gfin_ref[...])

    steps = [gates, merge]
    for c0, c1 in ff_chunks:
        steps += [functools.partial(f, c0, c1) for f in (gate, up, down)]
    return steps + [out]


def _merge_ffn_kernel(*refs, ff_chunks):
    for step in _ffn_steps(*refs, ff_chunks):
        step()


def _suffix_sums_lanes(x):
    lane = lax.broadcasted_iota(jnp.int32, x.shape, 1)
    y = x
    k = 1
    while k < LANES:
        shifted = pltpu.roll(y, LANES - k, axis=1)
        y = y + jnp.where(lane < LANES - k, shifted, 0.0)
        k *= 2
    return y


def _decode_pages(k_pages, v_pages, qb_ref, bias_ref, c_ref, acc_ref, w_ref):
    n = k_pages.shape[0]
    zs = []
    for i in range(n):
        rows = []
        for h in range(SB_HEADS):
            hs = slice(h * HEAD_DIM, (h + 1) * HEAD_DIM)
            rows.append(jnp.sum(k_pages[i, hs, :] * qb_ref[hs, :], axis=0, keepdims=True))
        zs.append(jnp.concatenate(rows, axis=0) + bias_ref[...])
    z = jnp.concatenate(zs, axis=0)
    lb, l1m = _log_sigmoid_pair(z)
    later = _suffix_sums_lanes(l1m) - l1m
    tot = jnp.sum(l1m, axis=1, keepdims=True)
    c = c_ref[...]
    for i in range(n):
        rows = slice(i * SB_HEADS, (i + 1) * SB_HEADS)
        w_ref[rows, :] = jnp.exp(lb[rows] + later[rows] + c)
        c = c + tot[rows]
    c_ref[...] = c
    for h in range(SB_HEADS):
        hs = slice(h * HEAD_DIM, (h + 1) * HEAD_DIM)
        a = acc_ref[hs, :]
        for i in range(n):
            r = i * SB_HEADS + h
            a = a + v_pages[i, hs, :] * w_ref[r:r + 1, :]
        acc_ref[hs, :] = a


def _lane_sums_row(acc, ones):
    a0 = acc.astype(BF16)
    r1 = acc - a0.astype(F32)
    a1 = r1.astype(BF16)
    a2 = (r1 - a1.astype(F32)).astype(BF16)
    tot = (lax.dot_general(ones, a0, _NT, preferred_element_type=F32)
           + lax.dot_general(ones, a1, _NT, preferred_element_type=F32)
           + lax.dot_general(ones, a2, _NT, preferred_element_type=F32))
    return tot[0:1]


def _merge_ffn_decode_kernel(pt_ref, *refs, ff_chunks, n_pages):
    ffn_refs, (qb_ref, bias_ref, ones_ref, kc_hbm, vc_hbm) = refs[:13], refs[13:18]
    y_ref, os_ref, kbuf, vbuf, sem, c_ref, acc_ref, w_ref = refs[18:]
    seq = pl.program_id(0)
    n_seq = pl.num_programs(0)
    n_chunks = n_pages // DEC_CHUNK
    ahead = DEC_SLOTS - 1

    def copies(g):
        sq = jnp.minimum(g // n_chunks, n_seq - 1)
        chunk = g % n_chunks
        slot = g % DEC_SLOTS
        out = []
        for i in range(DEC_CHUNK):
            page = pt_ref[sq * n_pages + (n_pages - 1 - (chunk * DEC_CHUNK + i))]
            out.append(pltpu.make_async_copy(kc_hbm.at[page], kbuf.at[slot, i], sem.at[0, slot]))
            out.append(pltpu.make_async_copy(vc_hbm.at[page], vbuf.at[slot, i], sem.at[1, slot]))
        return out

    g0 = seq * n_chunks

    @pl.when(seq == 0)
    def _():
        for g in range(ahead):
            for cp in copies(g):
                cp.start()

    c_ref[...] = jnp.zeros_like(c_ref)
    acc_ref[...] = jnp.zeros_like(acc_ref)
    steps = _ffn_steps(*ffn_refs, y_ref, ff_chunks)
    assert len(steps) == n_chunks + 1
    for chunk in range(n_chunks):
        g = g0 + chunk
        for cp in copies(g + ahead):
            cp.start()
        for cp in copies(g):
            cp.wait()
        slot = g % DEC_SLOTS
        _decode_pages(kbuf.at[slot], vbuf.at[slot], qb_ref, bias_ref, c_ref, acc_ref, w_ref)
        steps[chunk]()
    steps[n_chunks]()
    os_ref[0] = _lane_sums_row(acc_ref[...], ones_ref[...]).astype(os_ref.dtype)

    @pl.when(seq == n_seq - 1)
    def _():
        for k in range(ahead):
            for cp in copies(n_seq * n_chunks + k):
                cp.wait()


def _ff_chunks(d_ff, n=2):
    tiles = pl.cdiv(d_ff, MXU_DIM)
    cuts = [min(pl.cdiv(tiles * i, n) * MXU_DIM, d_ff) for i in range(n + 1)]
    return tuple(zip(cuts[:-1], cuts[1:]))


def _merge_ffn_call(x, o, yc, gmix, wg, wa, wc, wo, gffn, wgate, wup, wdown, gfin, tm):
    m = x.shape[0]
    d_ff = wgate.shape[1]
    row_spec = lambda width: pl.BlockSpec((tm, width), lambda i: (i, 0))
    return pl.pallas_call(
        functools.partial(_merge_ffn_kernel, ff_chunks=_ff_chunks(d_ff)),
        grid=(m // tm,),
        in_specs=[
            row_spec(D_MODEL), row_spec(D_ATT), row_spec(D_CONV),
            _const_spec((1, D_MODEL)),
            _const_spec(wg.shape), _const_spec(wa.shape), _const_spec(wc.shape), _const_spec(wo.shape),
            _const_spec((1, D_MODEL)),
            _const_spec(wgate.shape), _const_spec(wup.shape), _const_spec(wdown.shape),
            _const_spec((1, D_MODEL)),
        ],
        out_specs=row_spec(D_MODEL),
        out_shape=jax.ShapeDtypeStruct((m, D_MODEL), F32),
        compiler_params=pltpu.CompilerParams(
            dimension_semantics=("arbitrary",), vmem_limit_bytes=VMEM_LIMIT),
        name="merge_ffn",
    )(x, o, yc, gmix, wg, wa, wc, wo, gffn, wgate, wup, wdown, gfin)


def _merge_ffn_decode_call(page_table, ffn_args, qb, bias_b, ones, cache_kt, cache_vt, tm):
    x, wgate = ffn_args[0], ffn_args[9]
    m = x.shape[0]
    n_dec, n_pages = page_table.shape
    assert m // tm == n_dec
    row_spec = lambda width: pl.BlockSpec((tm, width), lambda i, pt: (i, 0))
    const = lambda a: _const_spec(a.shape)
    any_spec = pl.BlockSpec(memory_space=pl.ANY)
    grid_spec = pltpu.PrefetchScalarGridSpec(
        num_scalar_prefetch=1,
        grid=(n_dec,),
        in_specs=[row_spec(D_MODEL), row_spec(D_ATT), row_spec(D_CONV)] + [const(a) for a in ffn_args[3:]] + [
            pl.BlockSpec((None, D_ATT, LANES), lambda i, pt: (i, 0, 0)),
            const(bias_b), const(ones), any_spec, any_spec,
        ],
        out_specs=(row_spec(D_MODEL), pl.BlockSpec((1, 1, D_ATT), lambda i, pt: (i, 0, 0))),
        scratch_shapes=[
            pltpu.VMEM((DEC_SLOTS, DEC_CHUNK, D_ATT, PAGE), F32),
            pltpu.VMEM((DEC_SLOTS, DEC_CHUNK, D_ATT, PAGE), F32),
            pltpu.SemaphoreType.DMA((2, DEC_SLOTS)),
            pltpu.VMEM((SB_HEADS, LANES), F32),
            pltpu.VMEM((D_ATT, PAGE), F32),
            pltpu.VMEM((DEC_CHUNK * SB_HEADS, PAGE), F32),
        ],
    )
    return pl.pallas_call(
        functools.partial(_merge_ffn_decode_kernel, ff_chunks=_ff_chunks(wgate.shape[1]), n_pages=n_pages),
        grid_spec=grid_spec,
        out_shape=(jax.ShapeDtypeStruct((m, D_MODEL), F32), jax.ShapeDtypeStruct((n_dec, 1, D_ATT), BF16)),
        compiler_params=pltpu.CompilerParams(
            dimension_semantics=("arbitrary",), vmem_limit_bytes=VMEM_LIMIT),
        name="merge_ffn_decode",
    )(page_table.reshape(-1), *ffn_args, qb, bias_b, ones, cache_kt, cache_vt)


def _sample_proj_kernel(x_ref, g_ref, w_ref, cw_ref, s0_ref, s1_ref, q_ref, kf_ref, vf_ref, yc_ref, u_ref):
    h = _rms(x_ref[...], g_ref[...]).astype(BF16)
    p = jnp.dot(h, w_ref[...], preferred_element_type=F32)
    q_ref[...] = p[:, 0:D_ATT] * (HEAD_DIM ** -0.5)
    kf_ref[...] = p[:, D_ATT:2 * D_ATT]
    vf_ref[...] = p[:, 2 * D_ATT:3 * D_ATT]
    bg = p[:, 3 * D_ATT:3 * D_ATT + D_CONV]
    u = p[:, 3 * D_ATT + D_CONV:3 * D_ATT + 2 * D_CONV] * p[:, 3 * D_ATT + 2 * D_CONV:3 * D_ATT + 3 * D_CONV]
    cw = cw_ref[...]
    yc_ref[...] = (bg * (cw[0:1] * s0_ref[...] + cw[1:2] * s1_ref[...] + cw[2:3] * u)).astype(BF16)
    u_ref[...] = u


def _sample_proj_call(x, g, w, cw, s0, s1):
    n = x.shape[0]
    full = lambda shape: pl.BlockSpec(shape, lambda i: (0,) * len(shape))
    outs = (
        jax.ShapeDtypeStruct((n, D_ATT), F32),
        jax.ShapeDtypeStruct((n, D_ATT), F32),
        jax.ShapeDtypeStruct((n, D_ATT), F32),
        jax.ShapeDtypeStruct((n, D_CONV), BF16),
        jax.ShapeDtypeStruct((n, D_CONV), F32),
    )
    return pl.pallas_call(
        _sample_proj_kernel,
        grid=(1,),
        in_specs=[full(x.shape), full(g.shape), full(w.shape), full(cw.shape), full(s0.shape), full(s1.shape)],
        out_specs=tuple(full(o.shape) for o in outs),
        out_shape=outs,
        compiler_params=pltpu.CompilerParams(vmem_limit_bytes=VMEM_LIMIT),
        name="sample_proj",
    )(x, g, w, cw, s0, s1)


def _neg_suffix_matrix():
    j = np.arange(KBLK)[:, None]
    s = np.arange(KBLK)[None, :]
    blk = -np.concatenate([(j >= s).astype(np.float32), np.ones((KBLK, KBLK), np.float32)], axis=1)
    return jnp.asarray(np.concatenate([blk, blk], axis=0), dtype=F32)


def _pages_as_rows(cache):
    n_phys = cache.shape[0]
    return jnp.transpose(cache, (0, 2, 3, 1)).reshape(n_phys, D_ATT, PAGE)


def _state_from_rows(rows_t):
    b, _, n_pos = rows_t.shape
    return jnp.transpose(rows_t.reshape(b, SB_HEADS, HEAD_DIM, n_pos), (0, 3, 1, 2))[None]


def kernel(x_prompt, x_sample, cache_k, cache_v, state_conv, page_table, meta_tokens, norm_mix, w_in,
           sb_bias, conv_w, w_att_out, w_conv_out, w_o, norm_ffn, w_gate, w_up, w_down, norm_final):
    assert w_in.shape[0] == 1, "single-layer step: the meta rows' block output never reaches a returned array"
    n_prompt, seq, _ = x_prompt.shape
    n_dec, t_dec, _ = x_sample.shape
    assert t_dec == 1

    n_qkvc = 3 * D_ATT + 3 * D_CONV
    w_qkvc = w_in[0, :, :n_qkvc].astype(BF16)
    w_gates = w_in[0, :, n_qkvc:].astype(BF16)
    gmix = norm_mix[0][None, :]
    gffn = norm_ffn[0][None, :]
    gfin = norm_final[None, :]
    cw = conv_w[0]
    bias = sb_bias[0]
    ffn_w = (w_att_out[0].astype(BF16), w_conv_out[0].astype(BF16), w_o[0].astype(BF16), gffn,
             w_gate[0].astype(BF16), w_up[0].astype(BF16), w_down[0].astype(BF16), gfin)

    zero_halo = jnp.zeros((CONV_K - 1, D_CONV), F32)
    x_meta = jnp.pad(meta_tokens, ((KBLK - N_META, 0), (0, 0)))[None]
    no_front = jnp.zeros((D_ATT, LANES), F32)
    _, ktm_b, vm_b, ktm_f, vtm_f, _, u_meta = _proj_call(
        x_meta, gmix, w_qkvc, cw, zero_halo, no_front, no_front, 0, KBLK)

    q_b, kt_b, v_b, kt_f, vt_f, yc, conv_p = _proj_call(
        x_prompt, gmix, w_qkvc, cw, u_meta[0], ktm_f[0], vtm_f[0], N_META, PROJ_ROWS)
    o_att = _attn_call(bias * LOG2E, q_b, kt_b, v_b, ktm_b[0, 0], vm_b[0], _neg_suffix_matrix(), tq=ATTN_ROWS)

    st = state_conv[0]
    q_s, k_s, v_s, yc_s, u_s = _sample_proj_call(x_sample[:, 0], gmix, w_qkvc, cw, st[:, 0], st[:, 1])
    qb = jnp.broadcast_to(q_s[:, :, None], (n_dec, D_ATT, LANES))
    bias_b = jnp.broadcast_to(bias[:, None], (SB_HEADS, LANES))
    m_rows = n_prompt * seq
    prompt_rows = (x_prompt.reshape(m_rows, D_MODEL), o_att.reshape(m_rows, D_ATT), yc.reshape(m_rows, D_CONV))
    y_p, o_s = _merge_ffn_decode_call(
        page_table, prompt_rows + (gmix, w_gates) + ffn_w, qb, bias_b, jnp.ones((8, LANES), BF16),
        _pages_as_rows(cache_k[0]), _pages_as_rows(cache_v[0]), tm=m_rows // n_dec)
    y_s = _merge_ffn_call(x_sample[:, 0], o_s[:, 0], yc_s, gmix, w_gates, *ffn_w, tm=n_dec)

    return (
        y_p.reshape(n_prompt, seq, D_MODEL),
        y_s.reshape(n_dec, 1, D_MODEL),
        _state_from_rows(kt_f),
        _state_from_rows(vt_f),
        conv_p[None],
        k_s.reshape(1, n_dec, 1, SB_HEADS, HEAD_DIM),
        v_s.reshape(1, n_dec, 1, SB_HEADS, HEAD_DIM),
        jnp.stack([st[:, 1], u_s], axis=1)[None],
    )
```

```python
import functools

import jax
import jax.numpy as jnp
import numpy as np
from jax import lax
from jax.experimental import pallas as pl
from jax.experimental.pallas import tpu as pltpu

F32 = jnp.float32
BF16 = jnp.bfloat16

D_MODEL = 1024
N_META = 16
SB_HEADS = 8
HEAD_DIM = 64
D_ATT = SB_HEADS * HEAD_DIM
D_CONV = 512
CONV_K = 3
EPS = 1e-6
PAGE = 128

LANES = 128
MXU_DIM = 256
KBLK = 128
N_PAIR = D_ATT // LANES
BIAS_LIMBS = 3
PROJ_ROWS = 512
ATTN_ROWS = 1024
DEC_CHUNK = 8
DEC_SLOTS = 4
VMEM_LIMIT = 60 * 1024 * 1024
LOG2E = 1.4426950408889634

_NT = (((1,), (1,)), ((), ()))


def _rms(x, g):
    return x * lax.rsqrt(jnp.mean(x * x, axis=-1, keepdims=True) + EPS) * g


def _sigmoid(x):
    return 1.0 / (1.0 + jnp.exp(-x))


def _log_sigmoid_pair(z):
    l = jnp.log(1.0 + jnp.exp(-jnp.abs(z)))
    lb = jnp.minimum(z, 0.0) - l
    return lb, lb - z


def _const_spec(shape):
    nd = len(shape)
    return pl.BlockSpec(shape, lambda *_: (0,) * nd, pipeline_mode=pl.Buffered(1))


def _proj_kernel(x_ref, g_ref, w_ref, cw_ref, halo_ref, fk_ref, fv_ref,
                 q_ref, ktb_ref, vb_ref, ktf_ref, vtf_ref, yc_ref, st_ref,
                 carry_ref, kcar_ref, vcar_ref, *, n_front):
    m = pl.program_id(1)
    tm = x_ref.shape[1]

    def put_state(ref, car_ref, rows):
        if n_front == 0:
            ref[0] = rows.T
            return
        ref[0] = jnp.concatenate([car_ref[...], rows[:tm - n_front]], axis=0).T
        car_ref[...] = rows[tm - n_front:]

    def tile():
        h = _rms(x_ref[0], g_ref[...]).astype(BF16)
        p = jnp.dot(h, w_ref[...], preferred_element_type=F32)
        q_ref[0] = (p[:, 0:D_ATT] * (HEAD_DIM ** -0.5 * LOG2E)).astype(BF16)
        k = p[:, D_ATT:2 * D_ATT]
        v = p[:, 2 * D_ATT:3 * D_ATT]
        put_state(ktf_ref, kcar_ref, k)
        put_state(vtf_ref, vcar_ref, v)
        ktb = k.T.astype(BF16)
        for c in range(tm // KBLK):
            ktb_ref[0, c] = ktb[:, c * KBLK:(c + 1) * KBLK]
        vb_ref[0] = v.astype(BF16)

        bg = p[:, 3 * D_ATT:3 * D_ATT + D_CONV]
        u = p[:, 3 * D_ATT + D_CONV:3 * D_ATT + 2 * D_CONV] * p[:, 3 * D_ATT + 2 * D_CONV:3 * D_ATT + 3 * D_CONV]
        prev2 = carry_ref[0:1, :]
        prev1 = carry_ref[1:2, :]
        row = lax.broadcasted_iota(jnp.int32, u.shape, 0)
        u1 = jnp.where(row == 0, prev1, pltpu.roll(u, 1, 0))
        u2 = jnp.where(row == 0, prev2, jnp.where(row == 1, prev1, pltpu.roll(u, 2, 0)))
        cw = cw_ref[...]
        yc = bg * (cw[0:1] * u2 + cw[1:2] * u1 + cw[2:3] * u)
        yc_ref[0] = yc.astype(BF16)
        tail = u[tm - 2:tm, :]
        carry_ref[0:2, :] = tail
        st_ref[0] = tail

    @pl.when(m == 0)
    def _():
        carry_ref[0:2, :] = halo_ref[...]
        if n_front:
            kcar_ref[...] = fk_ref[...].T[LANES - n_front:, :]
            vcar_ref[...] = fv_ref[...].T[LANES - n_front:, :]

    if n_front == 0:
        tile()
    else:
        n_tiles = pl.num_programs(1) - 1
        pl.when(m < n_tiles)(tile)

        @pl.when(m == n_tiles)
        def _():
            pad = jnp.zeros((LANES - n_front, D_ATT), F32)
            ktf_ref[0, :, :LANES] = jnp.concatenate([kcar_ref[...], pad], axis=0).T
            vtf_ref[0, :, :LANES] = jnp.concatenate([vcar_ref[...], pad], axis=0).T


def _proj_call(x, g, w, cw, halo, front_k, front_v, n_front, tm):
    b, t, _ = x.shape
    assert tm % KBLK == 0
    n_tiles = t // tm
    tile_idx = (lambda j: jnp.minimum(j, n_tiles - 1)) if n_front else (lambda j: j)
    row_spec = lambda width: pl.BlockSpec((1, tm, width), lambda i, j: (i, tile_idx(j), 0))
    col_spec = pl.BlockSpec((1, D_ATT, tm), lambda i, j: (i, 0, j))
    outs = (
        jax.ShapeDtypeStruct((b, t, D_ATT), BF16),
        jax.ShapeDtypeStruct((b, t // KBLK, D_ATT, KBLK), BF16),
        jax.ShapeDtypeStruct((b, t, D_ATT), BF16),
        jax.ShapeDtypeStruct((b, D_ATT, n_front + t), F32),
        jax.ShapeDtypeStruct((b, D_ATT, n_front + t), F32),
        jax.ShapeDtypeStruct((b, t, D_CONV), BF16),
        jax.ShapeDtypeStruct((b, CONV_K - 1, D_CONV), F32),
    )
    return pl.pallas_call(
        functools.partial(_proj_kernel, n_front=n_front),
        grid=(b, n_tiles + (1 if n_front else 0)),
        in_specs=[
            row_spec(D_MODEL),
            _const_spec((1, D_MODEL)),
            _const_spec(w.shape),
            _const_spec((CONV_K, D_CONV)),
            _const_spec((CONV_K - 1, D_CONV)),
            _const_spec((D_ATT, LANES)),
            _const_spec((D_ATT, LANES)),
        ],
        out_specs=(
            row_spec(D_ATT),
            pl.BlockSpec((1, tm // KBLK, D_ATT, KBLK), lambda i, j: (i, tile_idx(j), 0, 0)),
            row_spec(D_ATT),
            col_spec, col_spec,
            row_spec(D_CONV),
            pl.BlockSpec((1, CONV_K - 1, D_CONV), lambda i, j: (i, 0, 0)),
        ),
        out_shape=outs,
        scratch_shapes=[pltpu.VMEM((8, D_CONV), F32), pltpu.VMEM((max(n_front, 8), D_ATT), F32),
                        pltpu.VMEM((max(n_front, 8), D_ATT), F32)],
        compiler_params=pltpu.CompilerParams(
            dimension_semantics=("arbitrary", "arbitrary"), vmem_limit_bytes=VMEM_LIMIT),
        name="proj_conv",
    )(x, g, w, cw, halo, front_k, front_v)


def _neg_log2_one_minus_beta_split(z, mask=None):
    neg_abs = pltpu.bitcast(pltpu.bitcast(z, jnp.uint32) | jnp.uint32(0x80000000), F32)
    p = jnp.maximum(z, 0.0) + jnp.log(1.0 + jnp.exp2(neg_abs)) * LOG2E
    if mask is not None:
        p = jnp.where(mask, p, 0.0)
    hi = pltpu.bitcast(pltpu.bitcast(p, jnp.uint32) & jnp.uint32(0xFFFF0000), F32)
    return hi, p - hi


def _sb_logits(z2, u2, mask):
    ys, tots = [], []
    for hh in range(2):
        z = z2[:, hh * KBLK:(hh + 1) * KBLK]
        hi, lo = _neg_log2_one_minus_beta_split(z, mask)
        s_ext = jnp.dot(jnp.concatenate([hi, lo], axis=1), u2, preferred_element_type=F32)
        ys.append(z + s_ext[:, :KBLK])
        tots.append(s_ext[:, KBLK:])
    return ys, tots


def _sb_weights(ys, tots, c_ref, rows, mask):
    ws = []
    for hh in range(2):
        c = c_ref[hh, rows, :]
        w = jnp.exp2(ys[hh] + c)
        if mask is not None:
            w = jnp.where(mask, w, 0.0)
        if tots is not None:
            c_ref[hh, rows, :] = c + tots[hh]
        ws.append(w.astype(BF16))
    return jnp.concatenate(ws, axis=1)


class _QueryTileSweep:
    def __init__(self, qt, q_ref, o_ref, shared, scratch):
        self.qt, self.q_ref, self.o_ref = qt, q_ref, o_ref
        self.kt_ref, self.v_ref, self.kmp_ref, self.vmp_ref, self.kb_ref, self.u2, self.um_ref = shared
        self.qx_ref, self.c_ref, self.acc_ref, self.zn_ref, self.wl_ref, self.ym_ref = scratch
        self.tq = q_ref.shape[0]
        self.nsub = self.tq // KBLK
        self.full = slice(0, self.tq)

    def scores(self, rows, kt, kb):
        zero = jnp.zeros((HEAD_DIM, KBLK), BF16)
        k2 = jnp.concatenate([jnp.concatenate([kt[:HEAD_DIM], zero], axis=0),
                              jnp.concatenate([zero, kt[HEAD_DIM:]], axis=0)], axis=1)
        return jnp.dot(self.qx_ref[rows, :], jnp.concatenate([k2, kb], axis=0), preferred_element_type=F32)

    def add_pv(self, rows, w2, v):
        lane = lax.broadcasted_iota(jnp.int32, v.shape, 1)
        zero = jnp.zeros_like(v)
        v2 = jnp.concatenate([jnp.where(lane < HEAD_DIM, v, zero), jnp.where(lane >= HEAD_DIM, v, zero)], axis=0)
        self.acc_ref[rows, :] += jnp.dot(w2, v2, preferred_element_type=F32)

    def overlapping_blocks(self):
        tq, nsub, qt = self.tq, self.nsub, self.qt
        self.qx_ref[:, :LANES] = self.q_ref[...]
        lane = lax.broadcasted_iota(jnp.int32, (tq, LANES), 1)
        self.qx_ref[:, LANES:] = jnp.where(lane < BIAS_LIMBS, 1.0, 0.0).astype(BF16)
        self.c_ref[...] = jnp.zeros_like(self.c_ref)
        self.acc_ref[...] = jnp.zeros_like(self.acc_ref)
        for jj in reversed(range(nsub)):
            j = qt * nsub + jj
            rows = slice(jj * KBLK, tq)
            row = lax.broadcasted_iota(jnp.int32, (tq - jj * KBLK, KBLK), 0)
            col = lax.broadcasted_iota(jnp.int32, (tq - jj * KBLK, KBLK), 1)
            mask = col < row
            z2 = self.scores(rows, self.kt_ref[j], self.kb_ref[...])
            w2 = _sb_weights(*_sb_logits(z2, self.u2, mask), self.c_ref, rows, mask)
            if jj > 0:
                self.add_pv(rows, w2, self.v_ref[j])
            else:
                self.wl_ref[...] = w2
            yield
        self.zn_ref[...] = self.scores(self.full, self.kt_ref[jnp.maximum(qt * nsub - 1, 0)], self.kb_ref[...])
        z = jnp.dot(self.qx_ref[...], self.kmp_ref[...], preferred_element_type=F32)
        hi, lo = _neg_log2_one_minus_beta_split(z)
        self.ym_ref[...] = z + jnp.dot(jnp.concatenate([hi, lo], axis=1), self.um_ref[...],
                                       preferred_element_type=F32)
        yield

    def earlier_blocks(self):
        nsub, qt, full = self.nsub, self.qt, self.full

        def body(i, j_pending):
            self.add_pv(full, self.wl_ref[...], self.v_ref[j_pending])
            j0 = (qt - i) * nsub - 1
            for jj in range(nsub):
                j = j0 - jj
                z2 = self.zn_ref[...] if jj == 0 else self.scores(full, self.kt_ref[j], self.kb_ref[...])
                w2 = _sb_weights(*_sb_logits(z2, self.u2, None), self.c_ref, full, None)
                if jj < nsub - 1:
                    self.add_pv(full, w2, self.v_ref[j])
                else:
                    self.wl_ref[...] = w2
            self.zn_ref[...] = self.scores(full, self.kt_ref[jnp.maximum(j0 - nsub, 0)], self.kb_ref[...])
            return j0 - (nsub - 1)

        self.j_pending = lax.fori_loop(0, qt, body, qt * nsub)

    def meta_block(self):
        self.add_pv(self.full, self.wl_ref[...], self.v_ref[self.j_pending])
        lane = lax.broadcasted_iota(jnp.int32, (self.tq, LANES), 1)
        c = jnp.where(lane < N_META, self.c_ref[0], self.c_ref[1])
        w = jnp.exp2(self.ym_ref[...] + c).astype(BF16)
        self.acc_ref[...] += jnp.dot(w, self.vmp_ref[...], preferred_element_type=F32)
        self.o_ref[...] = self.acc_ref[...].astype(self.o_ref.dtype)


def _attn_kernel(qa_ref, qb_ref, kt_ref, v_ref, kmp_ref, vmp_ref, kb_ref, u2_ref, um_ref, o_ref, *scratch):
    j = pl.program_id(2)
    n_tiles = o_ref.shape[0]
    shared = (kt_ref, v_ref, kmp_ref, vmp_ref, kb_ref, u2_ref[...], um_ref)
    tiles = [
        _QueryTileSweep(j, qa_ref, o_ref.at[j], shared, [s.at[0] for s in scratch]),
        _QueryTileSweep(n_tiles - 1 - j, qb_ref, o_ref.at[n_tiles - 1 - j], shared, [s.at[1] for s in scratch]),
    ]
    for _ in zip(*[t.overlapping_blocks() for t in tiles]):
        pass
    for t in tiles:
        t.earlier_blocks()
    for t in tiles:
        t.meta_block()


def _bias_limbs(bias2):
    limbs = []
    r = bias2
    for _ in range(BIAS_LIMBS):
        limb = r.astype(BF16)
        limbs.append(limb)
        r = r - limb.astype(F32)
    return jnp.stack(limbs).reshape(BIAS_LIMBS, N_PAIR, 2)


def _bias_rows(bias2):
    rows = jnp.repeat(_bias_limbs(bias2), KBLK, axis=2)
    out = jnp.zeros((N_PAIR, LANES, 2 * KBLK), BF16)
    return out.at[:, :BIAS_LIMBS, :].set(jnp.transpose(rows, (1, 0, 2)))


def _meta_operands(bias2, ktm, vm):
    kt_meta = ktm[:, KBLK - N_META:].reshape(N_PAIR, 2, HEAD_DIM, N_META)
    v_meta = vm[KBLK - N_META:, :].reshape(N_META, N_PAIR, 2, HEAD_DIM)
    limbs = _bias_limbs(bias2)
    keys = jnp.zeros((N_PAIR, 2 * LANES, KBLK), BF16)
    keys = keys.at[:, LANES, 2 * N_META:].set(-1e30)
    values = jnp.zeros((N_PAIR, KBLK, LANES), BF16)
    for hh in range(2):
        lanes = slice(hh * N_META, (hh + 1) * N_META)
        dims = slice(hh * HEAD_DIM, (hh + 1) * HEAD_DIM)
        keys = keys.at[:, dims, lanes].set(kt_meta[:, hh])
        keys = keys.at[:, LANES:LANES + BIAS_LIMBS, lanes].set(
            jnp.broadcast_to(jnp.transpose(limbs[:, :, hh])[:, :, None], (N_PAIR, BIAS_LIMBS, N_META)))
        values = values.at[:, lanes, dims].set(jnp.transpose(v_meta[:, :, hh], (1, 0, 2)))
    j = np.arange(KBLK)[:, None]
    s = np.arange(KBLK)[None, :]
    later = (j >= s) & (j // N_META == s // N_META) & (j < 2 * N_META)
    suffix = jnp.asarray(-np.concatenate([later, later], axis=0).astype(np.float32))
    return keys, values, suffix


def _attn_call(bias2, q, kt, v, ktm, vm, u2, tq):
    b, t, _ = q.shape
    nkb = t // KBLK
    n_tiles = t // tq
    assert n_tiles % 2 == 0
    meta_keys, meta_values, meta_suffix = _meta_operands(bias2, ktm, vm)
    pair_block = lambda rows, cols: pl.BlockSpec((None, rows, cols), lambda i, p, j: (p, 0, 0))
    tile_a = pl.BlockSpec((None, tq, LANES), lambda i, p, j: (i, j, p))
    tile_b = pl.BlockSpec((None, tq, LANES), lambda i, p, j: (i, n_tiles - 1 - j, p))
    return pl.pallas_call(
        _attn_kernel,
        grid=(b, N_PAIR, n_tiles // 2),
        in_specs=[
            tile_a, tile_b,
            pl.BlockSpec((None, nkb, LANES, KBLK), lambda i, p, j: (i, 0, p, 0)),
            pl.BlockSpec((None, nkb, KBLK, LANES), lambda i, p, j: (i, 0, 0, p)),
            pair_block(2 * LANES, KBLK),
            pair_block(KBLK, LANES),
            pair_block(LANES, 2 * KBLK),
            pl.BlockSpec((2 * KBLK, 2 * KBLK), lambda i, p, j: (0, 0)),
            pl.BlockSpec((2 * KBLK, KBLK), lambda i, p, j: (0, 0)),
        ],
        out_specs=pl.BlockSpec((None, n_tiles, tq, LANES), lambda i, p, j: (i, 0, 0, p)),
        out_shape=jax.ShapeDtypeStruct((b, n_tiles, tq, D_ATT), BF16),
        scratch_shapes=[pltpu.VMEM((2, tq, 2 * LANES), BF16), pltpu.VMEM((2, 2, tq, KBLK), F32),
                        pltpu.VMEM((2, tq, LANES), F32), pltpu.VMEM((2, tq, 2 * KBLK), F32),
                        pltpu.VMEM((2, tq, 2 * KBLK), BF16), pltpu.VMEM((2, tq, KBLK), F32)],
        compiler_params=pltpu.CompilerParams(
            dimension_semantics=("arbitrary", "arbitrary", "arbitrary"), vmem_limit_bytes=VMEM_LIMIT),
        name="sb_attn",
    )(q, q, kt, v.reshape(b, nkb, KBLK, D_ATT), meta_keys, meta_values, _bias_rows(bias2), u2, meta_suffix)


def _ffn_steps(x_ref, o_ref, yc_ref, gmix_ref, wg_ref, wa_ref, wc_ref, wo_ref,
               gffn_ref, wgate_ref, wup_ref, wdown_ref, gfin_ref, y_ref, ff_chunks):
    s = {}

    def gates():
        s["h"] = _rms(x_ref[...], gmix_ref[...]).astype(BF16)
        ga = jnp.dot(s["h"], wg_ref[:, :D_MODEL], preferred_element_type=F32)
        ya = jnp.dot(o_ref[...], wa_ref[...], preferred_element_type=F32)
        s["mrg"] = _sigmoid(ga) * ya

    def merge():
        gb = jnp.dot(s.pop("h"), wg_ref[:, D_MODEL:], preferred_element_type=F32)
        yb = jnp.dot(yc_ref[...], wc_ref[...], preferred_element_type=F32)
        mrg = s.pop("mrg") + _sigmoid(gb) * yb
        x1 = x_ref[...] + jnp.dot(mrg.astype(BF16), wo_ref[...], preferred_element_type=F32)
        s["h2"] = _rms(x1, gffn_ref[...]).astype(BF16)
        s["acc"] = x1

    def gate(c0, c1):
        s["a"] = jnp.dot(s["h2"], wgate_ref[:, c0:c1], preferred_element_type=F32)

    def up(c0, c1):
        a = s.pop("a")
        b = jnp.dot(s["h2"], wup_ref[:, c0:c1], preferred_element_type=F32)
        s["act"] = (a * _sigmoid(a) * b).astype(BF16)

    def down(c0, c1):
        s["acc"] = s["acc"] + jnp.dot(s.pop("act"), wdown_ref[c0:c1, :], preferred_element_type=F32)

    def out():
        y_ref[...] = _rms(s["acc"], gfin_ref[...])

    steps = [gates, merge]
    for c0, c1 in ff_chunks:
        steps += [functools.partial(f, c0, c1) for f in (gate, up, down)]
    return steps + [out]


def _merge_ffn_kernel(*refs, ff_chunks):
    for step in _ffn_steps(*refs, ff_chunks):
        step()


def _suffix_sums_lanes(x):
    lane = lax.broadcasted_iota(jnp.int32, x.shape, 1)
    y = x
    k = 1
    while k < LANES:
        shifted = pltpu.roll(y, LANES - k, axis=1)
        y = y + jnp.where(lane < LANES - k, shifted, 0.0)
        k *= 2
    return y


def _decode_pages(k_pages, v_pages, qb_ref, bias_ref, c_ref, acc_ref, w_ref):
    n = k_pages.shape[0]
    zs = []
    for i in range(n):
        rows = []
        for h in range(SB_HEADS):
            hs = slice(h * HEAD_DIM, (h + 1) * HEAD_DIM)
            rows.append(jnp.sum(k_pages[i, hs, :] * qb_ref[hs, :], axis=0, keepdims=True))
        zs.append(jnp.concatenate(rows, axis=0) + bias_ref[...])
    z = jnp.concatenate(zs, axis=0)
    lb, l1m = _log_sigmoid_pair(z)
    later = _suffix_sums_lanes(l1m) - l1m
    tot = jnp.sum(l1m, axis=1, keepdims=True)
    c = c_ref[...]
    for i in range(n):
        rows = slice(i * SB_HEADS, (i + 1) * SB_HEADS)
        w_ref[rows, :] = jnp.exp(lb[rows] + later[rows] + c)
        c = c + tot[rows]
    c_ref[...] = c
    for h in range(SB_HEADS):
        hs = slice(h * HEAD_DIM, (h + 1) * HEAD_DIM)
        a = acc_ref[hs, :]
        for i in range(n):
            r = i * SB_HEADS + h
            a = a + v_pages[i, hs, :] * w_ref[r:r + 1, :]
        acc_ref[hs, :] = a


def _lane_sums_row(acc, ones):
    a0 = acc.astype(BF16)
    r1 = acc - a0.astype(F32)
    a1 = r1.astype(BF16)
    a2 = (r1 - a1.astype(F32)).astype(BF16)
    tot = (lax.dot_general(ones, a0, _NT, preferred_element_type=F32)
           + lax.dot_general(ones, a1, _NT, preferred_element_type=F32)
           + lax.dot_general(ones, a2, _NT, preferred_element_type=F32))
    return tot[0:1]


def _merge_ffn_decode_kernel(pt_ref, *refs, ff_chunks, n_pages):
    ffn_refs, (qb_ref, bias_ref, ones_ref, kc_hbm, vc_hbm) = refs[:13], refs[13:18]
    y_ref, os_ref, kbuf, vbuf, sem, c_ref, acc_ref, w_ref = refs[18:]
    seq = pl.program_id(0)
    n_seq = pl.num_programs(0)
    n_chunks = n_pages // DEC_CHUNK
    ahead = DEC_SLOTS - 1

    def copies(g):
        sq = jnp.minimum(g // n_chunks, n_seq - 1)
        chunk = g % n_chunks
        slot = g % DEC_SLOTS
        out = []
        for i in range(DEC_CHUNK):
            page = pt_ref[sq * n_pages + (n_pages - 1 - (chunk * DEC_CHUNK + i))]
            out.append(pltpu.make_async_copy(kc_hbm.at[page], kbuf.at[slot, i], sem.at[0, slot]))
            out.append(pltpu.make_async_copy(vc_hbm.at[page], vbuf.at[slot, i], sem.at[1, slot]))
        return out

    g0 = seq * n_chunks

    @pl.when(seq == 0)
    def _():
        for g in range(ahead):
            for cp in copies(g):
                cp.start()

    c_ref[...] = jnp.zeros_like(c_ref)
    acc_ref[...] = jnp.zeros_like(acc_ref)
    steps = _ffn_steps(*ffn_refs, y_ref, ff_chunks)
    assert len(steps) == n_chunks + 1
    for chunk in range(n_chunks):
        g = g0 + chunk
        for cp in copies(g + ahead):
            cp.start()
        for cp in copies(g):
            cp.wait()
        slot = g % DEC_SLOTS
        _decode_pages(kbuf.at[slot], vbuf.at[slot], qb_ref, bias_ref, c_ref, acc_ref, w_ref)
        steps[chunk]()
    steps[n_chunks]()
    os_ref[0] = _lane_sums_row(acc_ref[...], ones_ref[...]).astype(os_ref.dtype)

    @pl.when(seq == n_seq - 1)
    def _():
        for k in range(ahead):
            for cp in copies(n_seq * n_chunks + k):
                cp.wait()


def _ff_chunks(d_ff, n=2):
    tiles = pl.cdiv(d_ff, MXU_DIM)
    cuts = [min(pl.cdiv(tiles * i, n) * MXU_DIM, d_ff) for i in range(n + 1)]
    return tuple(zip(cuts[:-1], cuts[1:]))


def _merge_ffn_call(x, o, yc, gmix, wg, wa, wc, wo, gffn, wgate, wup, wdown, gfin, tm):
    m = x.shape[0]
    d_ff = wgate.shape[1]
    row_spec = lambda width: pl.BlockSpec((tm, width), lambda i: (i, 0))
    return pl.pallas_call(
        functools.partial(_merge_ffn_kernel, ff_chunks=_ff_chunks(d_ff)),
        grid=(m // tm,),
        in_specs=[
            row_spec(D_MODEL), row_spec(D_ATT), row_spec(D_CONV),
            _const_spec((1, D_MODEL)),
            _const_spec(wg.shape), _const_spec(wa.shape), _const_spec(wc.shape), _const_spec(wo.shape),
            _const_spec((1, D_MODEL)),
            _const_spec(wgate.shape), _const_spec(wup.shape), _const_spec(wdown.shape),
            _const_spec((1, D_MODEL)),
        ],
        out_specs=row_spec(D_MODEL),
        out_shape=jax.ShapeDtypeStruct((m, D_MODEL), F32),
        compiler_params=pltpu.CompilerParams(
            dimension_semantics=("arbitrary",), vmem_limit_bytes=VMEM_LIMIT),
        name="merge_ffn",
    )(x, o, yc, gmix, wg, wa, wc, wo, gffn, wgate, wup, wdown, gfin)


def _merge_ffn_decode_call(page_table, ffn_args, qb, bias_b, ones, cache_kt, cache_vt, tm):
    x, wgate = ffn_args[0], ffn_args[9]
    m = x.shape[0]
    n_dec, n_pages = page_table.shape
    assert m // tm == n_dec
    row_spec = lambda width: pl.BlockSpec((tm, width), lambda i, pt: (i, 0))
    const = lambda a: _const_spec(a.shape)
    any_spec = pl.BlockSpec(memory_space=pl.ANY)
    grid_spec = pltpu.PrefetchScalarGridSpec(
        num_scalar_prefetch=1,
        grid=(n_dec,),
        in_specs=[row_spec(D_MODEL), row_spec(D_ATT), row_spec(D_CONV)] + [const(a) for a in ffn_args[3:]] + [
            pl.BlockSpec((None, D_ATT, LANES), lambda i, pt: (i, 0, 0)),
            const(bias_b), const(ones), any_spec, any_spec,
        ],
        out_specs=(row_spec(D_MODEL), pl.BlockSpec((1, 1, D_ATT), lambda i, pt: (i, 0, 0))),
        scratch_shapes=[
            pltpu.VMEM((DEC_SLOTS, DEC_CHUNK, D_ATT, PAGE), F32),
            pltpu.VMEM((DEC_SLOTS, DEC_CHUNK, D_ATT, PAGE), F32),
            pltpu.SemaphoreType.DMA((2, DEC_SLOTS)),
            pltpu.VMEM((SB_HEADS, LANES), F32),
            pltpu.VMEM((D_ATT, PAGE), F32),
            pltpu.VMEM((DEC_CHUNK * SB_HEADS, PAGE), F32),
        ],
    )
    return pl.pallas_call(
        functools.partial(_merge_ffn_decode_kernel, ff_chunks=_ff_chunks(wgate.shape[1]), n_pages=n_pages),
        grid_spec=grid_spec,
        out_shape=(jax.ShapeDtypeStruct((m, D_MODEL), F32), jax.ShapeDtypeStruct((n_dec, 1, D_ATT), BF16)),
        compiler_params=pltpu.CompilerParams(
            dimension_semantics=("arbitrary",), vmem_limit_bytes=VMEM_LIMIT),
        name="merge_ffn_decode",
    )(page_table.reshape(-1), *ffn_args, qb, bias_b, ones, cache_kt, cache_vt)


def _sample_proj_kernel(x_ref, g_ref, w_ref, cw_ref, s0_ref, s1_ref, q_ref, kf_ref, vf_ref, yc_ref, u_ref):
    h = _rms(x_ref[...], g_ref[...]).astype(BF16)
    p = jnp.dot(h, w_ref[...], preferred_element_type=F32)
    q_ref[...] = p[:, 0:D_ATT] * (HEAD_DIM ** -0.5)
    kf_ref[...] = p[:, D_ATT:2 * D_ATT]
    vf_ref[...] = p[:, 2 * D_ATT:3 * D_ATT]
    bg = p[:, 3 * D_ATT:3 * D_ATT + D_CONV]
    u = p[:, 3 * D_ATT + D_CONV:3 * D_ATT + 2 * D_CONV] * p[:, 3 * D_ATT + 2 * D_CONV:3 * D_ATT + 3 * D_CONV]
    cw = cw_ref[...]
    yc_ref[...] = (bg * (cw[0:1] * s0_ref[...] + cw[1:2] * s1_ref[...] + cw[2:3] * u)).astype(BF16)
    u_ref[...] = u


def _sample_proj_call(x, g, w, cw, s0, s1):
    n = x.shape[0]
    full = lambda shape: pl.BlockSpec(shape, lambda i: (0,) * len(shape))
    outs = (
        jax.ShapeDtypeStruct((n, D_ATT), F32),
        jax.ShapeDtypeStruct((n, D_ATT), F32),
        jax.ShapeDtypeStruct((n, D_ATT), F32),
        jax.ShapeDtypeStruct((n, D_CONV), BF16),
        jax.ShapeDtypeStruct((n, D_CONV), F32),
    )
    return pl.pallas_call(
        _sample_proj_kernel,
        grid=(1,),
        in_specs=[full(x.shape), full(g.shape), full(w.shape), full(cw.shape), full(s0.shape), full(s1.shape)],
        out_specs=tuple(full(o.shape) for o in outs),
        out_shape=outs,
        compiler_params=pltpu.CompilerParams(vmem_limit_bytes=VMEM_LIMIT),
        name="sample_proj",
    )(x, g, w, cw, s0, s1)


def _neg_suffix_matrix():
    j = np.arange(KBLK)[:, None]
    s = np.arange(KBLK)[None, :]
    blk = -np.concatenate([(j >= s).astype(np.float32), np.ones((KBLK, KBLK), np.float32)], axis=1)
    return jnp.asarray(np.concatenate([blk, blk], axis=0), dtype=F32)


def _pages_as_rows(cache):
    n_phys = cache.shape[0]
    return jnp.transpose(cache, (0, 2, 3, 1)).reshape(n_phys, D_ATT, PAGE)


def _state_from_rows(rows_t):
    b, _, n_pos = rows_t.shape
    return jnp.transpose(rows_t.reshape(b, SB_HEADS, HEAD_DIM, n_pos), (0, 3, 1, 2))[None]


def kernel(x_prompt, x_sample, cache_k, cache_v, state_conv, page_table, meta_tokens, norm_mix, w_in,
           sb_bias, conv_w, w_att_out, w_conv_out, w_o, norm_ffn, w_gate, w_up, w_down, norm_final):
    assert w_in.shape[0] == 1, "single-layer step: the meta rows' block output never reaches a returned array"
    n_prompt, seq, _ = x_prompt.shape
    n_dec, t_dec, _ = x_sample.shape
    assert t_dec == 1

    n_qkvc = 3 * D_ATT + 3 * D_CONV
    w_qkvc = w_in[0, :, :n_qkvc].astype(BF16)
    w_gates = w_in[0, :, n_qkvc:].astype(BF16)
    gmix = norm_mix[0][None, :]
    gffn = norm_ffn[0][None, :]
    gfin = norm_final[None, :]
    cw = conv_w[0]
    bias = sb_bias[0]
    ffn_w = (w_att_out[0].astype(BF16), w_conv_out[0].astype(BF16), w_o[0].astype(BF16), gffn,
             w_gate[0].astype(BF16), w_up[0].astype(BF16), w_down[0].astype(BF16), gfin)

    zero_halo = jnp.zeros((CONV_K - 1, D_CONV), F32)
    x_meta = jnp.pad(meta_tokens, ((KBLK - N_META, 0), (0, 0)))[None]
    no_front = jnp.zeros((D_ATT, LANES), F32)
    _, ktm_b, vm_b, ktm_f, vtm_f, _, u_meta = _proj_call(
        x_meta, gmix, w_qkvc, cw, zero_halo, no_front, no_front, 0, KBLK)

    q_b, kt_b, v_b, kt_f, vt_f, yc, conv_p = _proj_call(
        x_prompt, gmix, w_qkvc, cw, u_meta[0], ktm_f[0], vtm_f[0], N_META, PROJ_ROWS)
    o_att = _attn_call(bias * LOG2E, q_b, kt_b, v_b, ktm_b[0, 0], vm_b[0], _neg_suffix_matrix(), tq=ATTN_ROWS)

    st = state_conv[0]
    q_s, k_s, v_s, yc_s, u_s = _sample_proj_call(x_sample[:, 0], gmix, w_qkvc, cw, st[:, 0], st[:, 1])
    qb = jnp.broadcast_to(q_s[:, :, None], (n_dec, D_ATT, LANES))
    bias_b = jnp.broadcast_to(bias[:, None], (SB_HEADS, LANES))
    m_rows = n_prompt * seq
    prompt_rows = (x_prompt.reshape(m_rows, D_MODEL), o_att.reshape(m_rows, D_ATT), yc.reshape(m_rows, D_CONV))
    y_p, o_s = _merge_ffn_decode_call(
        page_table, prompt_rows + (gmix, w_gates) + ffn_w, qb, bias_b, jnp.ones((8, LANES), BF16),
        _pages_as_rows(cache_k[0]), _pages_as_rows(cache_v[0]), tm=m_rows // n_dec)
    y_s = _merge_ffn_call(x_sample[:, 0], o_s[:, 0], yc_s, gmix, w_gates, *ffn_w, tm=n_dec)

    return (
        y_p.reshape(n_prompt, seq, D_MODEL),
        y_s.reshape(n_dec, 1, D_MODEL),
        _state_from_rows(kt_f),
        _state_from_rows(vt_f),
        conv_p[None],
        k_s.reshape(1, n_dec, 1, SB_HEADS, HEAD_DIM),
        v_s.reshape(1, n_dec, 1, SB_HEADS, HEAD_DIM),
        jnp.stack([st[:, 1], u_s], axis=1)[None],
    )
```

```python
import functools

import jax
import jax.numpy as jnp
import numpy as np
from jax import lax
from jax.experimental import pallas as pl
from jax.experimental.pallas import tpu as pltpu

F32 = jnp.float32
BF16 = jnp.bfloat16

D_MODEL = 1024
N_META = 16
SB_HEADS = 8
HEAD_DIM = 64
D_ATT = SB_HEADS * HEAD_DIM
D_CONV = 512
CONV_K = 3
EPS = 1e-6
PAGE = 128

LANES = 128
MXU_DIM = 256
KBLK = 128
N_PAIR = D_ATT // LANES
BIAS_LIMBS = 3
PROJ_ROWS = 512
ATTN_ROWS = 1024
DEC_CHUNK = 8
DEC_SLOTS = 3
VMEM_LIMIT = 60 * 1024 * 1024
LOG2E = 1.4426950408889634

_NT = (((1,), (1,)), ((), ()))


def _rms(x, g):
    return x * lax.rsqrt(jnp.mean(x * x, axis=-1, keepdims=True) + EPS) * g


def _sigmoid(x):
    return 1.0 / (1.0 + jnp.exp(-x))


def _log_sigmoid_pair(z):
    l = jnp.log(1.0 + jnp.exp(-jnp.abs(z)))
    lb = jnp.minimum(z, 0.0) - l
    return lb, lb - z


def _const_spec(shape):
    nd = len(shape)
    return pl.BlockSpec(shape, lambda *_: (0,) * nd, pipeline_mode=pl.Buffered(1))


def _proj_kernel(x_ref, g_ref, w_ref, cw_ref, halo_ref, fk_ref, fv_ref,
                 q_ref, ktb_ref, vb_ref, ktf_ref, vtf_ref, yc_ref, st_ref,
                 carry_ref, kcar_ref, vcar_ref, *, n_front):
    m = pl.program_id(1)
    tm = x_ref.shape[1]

    def put_state(ref, car_ref, rows):
        if n_front == 0:
            ref[0] = rows.T
            return
        ref[0] = jnp.concatenate([car_ref[...], rows[:tm - n_front]], axis=0).T
        car_ref[...] = rows[tm - n_front:]

    def tile():
        h = _rms(x_ref[0], g_ref[...]).astype(BF16)
        p = jnp.dot(h, w_ref[...], preferred_element_type=F32)
        q_ref[0] = (p[:, 0:D_ATT] * (HEAD_DIM ** -0.5 * LOG2E)).astype(BF16)
        k = p[:, D_ATT:2 * D_ATT]
        v = p[:, 2 * D_ATT:3 * D_ATT]
        put_state(ktf_ref, kcar_ref, k)
        put_state(vtf_ref, vcar_ref, v)
        ktb = k.T.astype(BF16)
        for c in range(tm // KBLK):
            ktb_ref[0, c] = ktb[:, c * KBLK:(c + 1) * KBLK]
        vb_ref[0] = v.astype(BF16)

        bg = p[:, 3 * D_ATT:3 * D_ATT + D_CONV]
        u = p[:, 3 * D_ATT + D_CONV:3 * D_ATT + 2 * D_CONV] * p[:, 3 * D_ATT + 2 * D_CONV:3 * D_ATT + 3 * D_CONV]
        prev2 = carry_ref[0:1, :]
        prev1 = carry_ref[1:2, :]
        row = lax.broadcasted_iota(jnp.int32, u.shape, 0)
        u1 = jnp.where(row == 0, prev1, pltpu.roll(u, 1, 0))
        u2 = jnp.where(row == 0, prev2, jnp.where(row == 1, prev1, pltpu.roll(u, 2, 0)))
        cw = cw_ref[...]
        yc = bg * (cw[0:1] * u2 + cw[1:2] * u1 + cw[2:3] * u)
        yc_ref[0] = yc.astype(BF16)
        tail = u[tm - 2:tm, :]
        carry_ref[0:2, :] = tail
        st_ref[0] = tail

    @pl.when(m == 0)
    def _():
        carry_ref[0:2, :] = halo_ref[...]
        if n_front:
            kcar_ref[...] = fk_ref[...].T[LANES - n_front:, :]
            vcar_ref[...] = fv_ref[...].T[LANES - n_front:, :]

    if n_front == 0:
        tile()
    else:
        n_tiles = pl.num_programs(1) - 1
        pl.when(m < n_tiles)(tile)

        @pl.when(m == n_tiles)
        def _():
            pad = jnp.zeros((LANES - n_front, D_ATT), F32)
            ktf_ref[0, :, :LANES] = jnp.concatenate([kcar_ref[...], pad], axis=0).T
            vtf_ref[0, :, :LANES] = jnp.concatenate([vcar_ref[...], pad], axis=0).T


def _proj_call(x, g, w, cw, halo, front_k, front_v, n_front, tm):
    b, t, _ = x.shape
    assert tm % KBLK == 0
    n_tiles = t // tm
    tile_idx = (lambda j: jnp.minimum(j, n_tiles - 1)) if n_front else (lambda j: j)
    row_spec = lambda width: pl.BlockSpec((1, tm, width), lambda i, j: (i, tile_idx(j), 0))
    col_spec = pl.BlockSpec((1, D_ATT, tm), lambda i, j: (i, 0, j))
    outs = (
        jax.ShapeDtypeStruct((b, t, D_ATT), BF16),
        jax.ShapeDtypeStruct((b, t // KBLK, D_ATT, KBLK), BF16),
        jax.ShapeDtypeStruct((b, t, D_ATT), BF16),
        jax.ShapeDtypeStruct((b, D_ATT, n_front + t), F32),
        jax.ShapeDtypeStruct((b, D_ATT, n_front + t), F32),
        jax.ShapeDtypeStruct((b, t, D_CONV), BF16),
        jax.ShapeDtypeStruct((b, CONV_K - 1, D_CONV), F32),
    )
    return pl.pallas_call(
        functools.partial(_proj_kernel, n_front=n_front),
        grid=(b, n_tiles + (1 if n_front else 0)),
        in_specs=[
            row_spec(D_MODEL),
            _const_spec((1, D_MODEL)),
            _const_spec(w.shape),
            _const_spec((CONV_K, D_CONV)),
            _const_spec((CONV_K - 1, D_CONV)),
            _const_spec((D_ATT, LANES)),
            _const_spec((D_ATT, LANES)),
        ],
        out_specs=(
            row_spec(D_ATT),
            pl.BlockSpec((1, tm // KBLK, D_ATT, KBLK), lambda i, j: (i, tile_idx(j), 0, 0)),
            row_spec(D_ATT),
            col_spec, col_spec,
            row_spec(D_CONV),
            pl.BlockSpec((1, CONV_K - 1, D_CONV), lambda i, j: (i, 0, 0)),
        ),
        out_shape=outs,
        scratch_shapes=[pltpu.VMEM((8, D_CONV), F32), pltpu.VMEM((max(n_front, 8), D_ATT), F32),
                        pltpu.VMEM((max(n_front, 8), D_ATT), F32)],
        compiler_params=pltpu.CompilerParams(
            dimension_semantics=("arbitrary", "arbitrary"), vmem_limit_bytes=VMEM_LIMIT),
        name="proj_conv",
    )(x, g, w, cw, halo, front_k, front_v)


def _neg_log2_one_minus_beta_split(z, mask=None):
    neg_abs = pltpu.bitcast(pltpu.bitcast(z, jnp.uint32) | jnp.uint32(0x80000000), F32)
    p = jnp.maximum(z, 0.0) + jnp.log(1.0 + jnp.exp2(neg_abs)) * LOG2E
    if mask is not None:
        p = jnp.where(mask, p, 0.0)
    hi = pltpu.bitcast(pltpu.bitcast(p, jnp.uint32) & jnp.uint32(0xFFFF0000), F32)
    return hi, p - hi


def _sb_logits(z2, u2, mask):
    ys, tots = [], []
    for hh in range(2):
        z = z2[:, hh * KBLK:(hh + 1) * KBLK]
        hi, lo = _neg_log2_one_minus_beta_split(z, mask)
        s_ext = jnp.dot(jnp.concatenate([hi, lo], axis=1), u2, preferred_element_type=F32)
        ys.append(z + s_ext[:, :KBLK])
        tots.append(s_ext[:, KBLK:])
    return ys, tots


def _sb_weights(ys, tots, c_ref, rows, mask):
    ws = []
    for hh in range(2):
        c = c_ref[hh, rows, :]
        w = jnp.exp2(ys[hh] + c)
        if mask is not None:
            w = jnp.where(mask, w, 0.0)
        if tots is not None:
            c_ref[hh, rows, :] = c + tots[hh]
        ws.append(w.astype(BF16))
    return jnp.concatenate(ws, axis=1)


class _QueryTileSweep:
    def __init__(self, qt, q_ref, o_ref, shared, scratch):
        self.qt, self.q_ref, self.o_ref = qt, q_ref, o_ref
        self.kt_ref, self.v_ref, self.kmp_ref, self.vmp_ref, self.kb_ref, self.u2, self.um_ref = shared
        self.qx_ref, self.c_ref, self.acc_ref, self.zn_ref, self.wl_ref, self.ym_ref = scratch
        self.tq = q_ref.shape[0]
        self.nsub = self.tq // KBLK
        self.full = slice(0, self.tq)
        self.row_halves = (slice(0, self.tq // 2), slice(self.tq // 2, self.tq))

    def scores(self, rows, kt, kb):
        zero = jnp.zeros((HEAD_DIM, KBLK), BF16)
        k2 = jnp.concatenate([jnp.concatenate([kt[:HEAD_DIM], zero], axis=0),
                              jnp.concatenate([zero, kt[HEAD_DIM:]], axis=0)], axis=1)
        return jnp.dot(self.qx_ref[rows, :], jnp.concatenate([k2, kb], axis=0), preferred_element_type=F32)

    def add_pv(self, rows, w2, v):
        lane = lax.broadcasted_iota(jnp.int32, v.shape, 1)
        zero = jnp.zeros_like(v)
        v2 = jnp.concatenate([jnp.where(lane < HEAD_DIM, v, zero), jnp.where(lane >= HEAD_DIM, v, zero)], axis=0)
        self.acc_ref[rows, :] += jnp.dot(w2, v2, preferred_element_type=F32)

    def overlapping_blocks(self):
        tq, nsub, qt = self.tq, self.nsub, self.qt
        self.qx_ref[:, :LANES] = self.q_ref[...]
        lane = lax.broadcasted_iota(jnp.int32, (tq, LANES), 1)
        self.qx_ref[:, LANES:] = jnp.where(lane < BIAS_LIMBS, 1.0, 0.0).astype(BF16)
        self.c_ref[...] = jnp.zeros_like(self.c_ref)
        self.acc_ref[...] = jnp.zeros_like(self.acc_ref)
        for jj in reversed(range(nsub)):
            j = qt * nsub + jj
            rows = slice(jj * KBLK, tq)
            row = lax.broadcasted_iota(jnp.int32, (tq - jj * KBLK, KBLK), 0)
            col = lax.broadcasted_iota(jnp.int32, (tq - jj * KBLK, KBLK), 1)
            mask = col < row
            z2 = self.scores(rows, self.kt_ref[j], self.kb_ref[...])
            w2 = _sb_weights(*_sb_logits(z2, self.u2, mask), self.c_ref, rows, mask)
            if jj > 0:
                self.add_pv(rows, w2, self.v_ref[j])
            else:
                self.wl_ref[...] = w2
            yield
        self.zn_ref[...] = self.scores(self.full, self.kt_ref[jnp.maximum(qt * nsub - 1, 0)], self.kb_ref[...])
        z = jnp.dot(self.qx_ref[...], self.kmp_ref[...], preferred_element_type=F32)
        hi, lo = _neg_log2_one_minus_beta_split(z)
        self.ym_ref[...] = z + jnp.dot(jnp.concatenate([hi, lo], axis=1), self.um_ref[...],
                                       preferred_element_type=F32)
        yield

    def earlier_blocks(self):
        nsub, qt, full = self.nsub, self.qt, self.full

        def body(i, j_pending):
            self.add_pv(full, self.wl_ref[...], self.v_ref[j_pending])
            j0 = (qt - i) * nsub - 1
            for jj in range(nsub):
                j = j0 - jj
                for rows in self.row_halves:
                    z2 = self.zn_ref[rows, :] if jj == 0 else self.scores(rows, self.kt_ref[j], self.kb_ref[...])
                    w2 = _sb_weights(*_sb_logits(z2, self.u2, None), self.c_ref, rows, None)
                    if jj < nsub - 1:
                        self.add_pv(rows, w2, self.v_ref[j])
                    else:
                        self.wl_ref[rows, :] = w2
            self.zn_ref[...] = self.scores(full, self.kt_ref[jnp.maximum(j0 - nsub, 0)], self.kb_ref[...])
            return j0 - (nsub - 1)

        self.j_pending = lax.fori_loop(0, qt, body, qt * nsub)

    def meta_block(self):
        self.add_pv(self.full, self.wl_ref[...], self.v_ref[self.j_pending])
        lane = lax.broadcasted_iota(jnp.int32, (self.tq, LANES), 1)
        c = jnp.where(lane < N_META, self.c_ref[0], self.c_ref[1])
        w = jnp.exp2(self.ym_ref[...] + c).astype(BF16)
        self.acc_ref[...] += jnp.dot(w, self.vmp_ref[...], preferred_element_type=F32)
        self.o_ref[...] = self.acc_ref[...].astype(self.o_ref.dtype)


def _attn_kernel(qa_ref, qb_ref, kt_ref, v_ref, kmp_ref, vmp_ref, kb_ref, u2_ref, um_ref, o_ref, *scratch):
    j = pl.program_id(2)
    n_tiles = o_ref.shape[0]
    shared = (kt_ref, v_ref, kmp_ref, vmp_ref, kb_ref, u2_ref[...], um_ref)
    tiles = [
        _QueryTileSweep(j, qa_ref, o_ref.at[j], shared, [s.at[0] for s in scratch]),
        _QueryTileSweep(n_tiles - 1 - j, qb_ref, o_ref.at[n_tiles - 1 - j], shared, [s.at[1] for s in scratch]),
    ]
    for _ in zip(*[t.overlapping_blocks() for t in tiles]):
        pass
    for t in tiles:
        t.earlier_blocks()
    for t in tiles:
        t.meta_block()


def _bias_limbs(bias2):
    limbs = []
    r = bias2
    for _ in range(BIAS_LIMBS):
        limb = r.astype(BF16)
        limbs.append(limb)
        r = r - limb.astype(F32)
    return jnp.stack(limbs).reshape(BIAS_LIMBS, N_PAIR, 2)


def _bias_rows(bias2):
    rows = jnp.repeat(_bias_limbs(bias2), KBLK, axis=2)
    out = jnp.zeros((N_PAIR, LANES, 2 * KBLK), BF16)
    return out.at[:, :BIAS_LIMBS, :].set(jnp.transpose(rows, (1, 0, 2)))


def _meta_operands(bias2, ktm, vm):
    kt_meta = ktm[:, KBLK - N_META:].reshape(N_PAIR, 2, HEAD_DIM, N_META)
    v_meta = vm[KBLK - N_META:, :].reshape(N_META, N_PAIR, 2, HEAD_DIM)
    limbs = _bias_limbs(bias2)
    keys = jnp.zeros((N_PAIR, 2 * LANES, KBLK), BF16)
    keys = keys.at[:, LANES, 2 * N_META:].set(-1e30)
    values = jnp.zeros((N_PAIR, KBLK, LANES), BF16)
    for hh in range(2):
        lanes = slice(hh * N_META, (hh + 1) * N_META)
        dims = slice(hh * HEAD_DIM, (hh + 1) * HEAD_DIM)
        keys = keys.at[:, dims, lanes].set(kt_meta[:, hh])
        keys = keys.at[:, LANES:LANES + BIAS_LIMBS, lanes].set(
            jnp.broadcast_to(jnp.transpose(limbs[:, :, hh])[:, :, None], (N_PAIR, BIAS_LIMBS, N_META)))
        values = values.at[:, lanes, dims].set(jnp.transpose(v_meta[:, :, hh], (1, 0, 2)))
    j = np.arange(KBLK)[:, None]
    s = np.arange(KBLK)[None, :]
    later = (j >= s) & (j // N_META == s // N_META) & (j < 2 * N_META)
    suffix = jnp.asarray(-np.concatenate([later, later], axis=0).astype(np.float32))
    return keys, values, suffix


def _attn_call(bias2, q, kt, v, ktm, vm, u2, tq):
    b, t, _ = q.shape
    nkb = t // KBLK
    n_tiles = t // tq
    assert n_tiles % 2 == 0
    meta_keys, meta_values, meta_suffix = _meta_operands(bias2, ktm, vm)
    pair_block = lambda rows, cols: pl.BlockSpec((None, rows, cols), lambda i, p, j: (p, 0, 0))
    tile_a = pl.BlockSpec((None, tq, LANES), lambda i, p, j: (i, j, p))
    tile_b = pl.BlockSpec((None, tq, LANES), lambda i, p, j: (i, n_tiles - 1 - j, p))
    return pl.pallas_call(
        _attn_kernel,
        grid=(b, N_PAIR, n_tiles // 2),
        in_specs=[
            tile_a, tile_b,
            pl.BlockSpec((None, nkb, LANES, KBLK), lambda i, p, j: (i, 0, p, 0)),
            pl.BlockSpec((None, nkb, KBLK, LANES), lambda i, p, j: (i, 0, 0, p)),
            pair_block(2 * LANES, KBLK),
            pair_block(KBLK, LANES),
            pair_block(LANES, 2 * KBLK),
            pl.BlockSpec((2 * KBLK, 2 * KBLK), lambda i, p, j: (0, 0)),
            pl.BlockSpec((2 * KBLK, KBLK), lambda i, p, j: (0, 0)),
        ],
        out_specs=pl.BlockSpec((None, n_tiles, tq, LANES), lambda i, p, j: (i, 0, 0, p)),
        out_shape=jax.ShapeDtypeStruct((b, n_tiles, tq, D_ATT), BF16),
        scratch_shapes=[pltpu.VMEM((2, tq, 2 * LANES), BF16), pltpu.VMEM((2, 2, tq, KBLK), F32),
                        pltpu.VMEM((2, tq, LANES), F32), pltpu.VMEM((2, tq, 2 * KBLK), F32),
                        pltpu.VMEM((2, tq, 2 * KBLK), BF16), pltpu.VMEM((2, tq, KBLK), F32)],
        compiler_params=pltpu.CompilerParams(
            dimension_semantics=("arbitrary", "arbitrary", "arbitrary"), vmem_limit_bytes=VMEM_LIMIT),
        name="sb_attn",
    )(q, q, kt, v.reshape(b, nkb, KBLK, D_ATT), meta_keys, meta_values, _bias_rows(bias2), u2, meta_suffix)


def _ffn_steps(x_ref, o_ref, yc_ref, gmix_ref, wg_ref, wa_ref, wc_ref, wo_ref,
               gffn_ref, wgate_ref, wup_ref, wdown_ref, gfin_ref, y_ref, ff_chunks):
    s = {}

    def gates():
        s["h"] = _rms(x_ref[...], gmix_ref[...]).astype(BF16)
        ga = jnp.dot(s["h"], wg_ref[:, :D_MODEL], preferred_element_type=F32)
        ya = jnp.dot(o_ref[...], wa_ref[...], preferred_element_type=F32)
        s["mrg"] = _sigmoid(ga) * ya

    def merge():
        gb = jnp.dot(s.pop("h"), wg_ref[:, D_MODEL:], preferred_element_type=F32)
        yb = jnp.dot(yc_ref[...], wc_ref[...], preferred_element_type=F32)
        mrg = s.pop("mrg") + _sigmoid(gb) * yb
        x1 = x_ref[...] + jnp.dot(mrg.astype(BF16), wo_ref[...], preferred_element_type=F32)
        s["h2"] = _rms(x1, gffn_ref[...]).astype(BF16)
        s["acc"] = x1

    def gate(c0, c1):
        s["a"] = jnp.dot(s["h2"], wgate_ref[:, c0:c1], preferred_element_type=F32)

    def up(c0, c1):
        a = s.pop("a")
        b = jnp.dot(s["h2"], wup_ref[:, c0:c1], preferred_element_type=F32)
        s["act"] = (a * _sigmoid(a) * b).astype(BF16)

    def down(c0, c1):
        s["acc"] = s["acc"] + jnp.dot(s.pop("act"), wdown_ref[c0:c1, :], preferred_element_type=F32)

    def out():
        y_ref[...] = _rms(s["acc"], gfin_ref[...])

    steps = [gates, merge]
    for c0, c1 in ff_chunks:
        steps += [functools.partial(f, c0, c1) for f in (gate, up, down)]
    return steps + [out]


def _merge_ffn_kernel(*refs, ff_chunks):
    for step in _ffn_steps(*refs, ff_chunks):
        step()


def _suffix_sums_lanes(x):
    lane = lax.broadcasted_iota(jnp.int32, x.shape, 1)
    y = x
    k = 1
    while k < LANES:
        shifted = pltpu.roll(y, LANES - k, axis=1)
        y = y + jnp.where(lane < LANES - k, shifted, 0.0)
        k *= 2
    return y


def _decode_pages(k_pages, v_pages, qb_ref, bias_ref, c_ref, acc_ref, w_ref):
    n = k_pages.shape[0]
    zs = []
    for i in range(n):
        rows = []
        for h in range(SB_HEADS):
            hs = slice(h * HEAD_DIM, (h + 1) * HEAD_DIM)
            rows.append(jnp.sum(k_pages[i, hs, :] * qb_ref[hs, :], axis=0, keepdims=True))
        zs.append(jnp.concatenate(rows, axis=0) + bias_ref[...])
    z = jnp.concatenate(zs, axis=0)
    lb, l1m = _log_sigmoid_pair(z)
    later = _suffix_sums_lanes(l1m) - l1m
    tot = jnp.sum(l1m, axis=1, keepdims=True)
    c = c_ref[...]
    for i in range(n):
        rows = slice(i * SB_HEADS, (i + 1) * SB_HEADS)
        w_ref[rows, :] = jnp.exp(lb[rows] + later[rows] + c)
        c = c + tot[rows]
    c_ref[...] = c
    for h in range(SB_HEADS):
        hs = slice(h * HEAD_DIM, (h + 1) * HEAD_DIM)
        a = acc_ref[hs, :]
        for i in range(n):
            r = i * SB_HEADS + h
            a = a + v_pages[i, hs, :] * w_ref[r:r + 1, :]
        acc_ref[hs, :] = a


def _lane_sums_row(acc, ones):
    a0 = acc.astype(BF16)
    r1 = acc - a0.astype(F32)
    a1 = r1.astype(BF16)
    a2 = (r1 - a1.astype(F32)).astype(BF16)
    tot = (lax.dot_general(ones, a0, _NT, preferred_element_type=F32)
           + lax.dot_general(ones, a1, _NT, preferred_element_type=F32)
           + lax.dot_general(ones, a2, _NT, preferred_element_type=F32))
    return tot[0:1]


def _merge_ffn_decode_kernel(pt_ref, *refs, ff_chunks, n_pages):
    ffn_refs, (qb_ref, bias_ref, ones_ref, kc_hbm, vc_hbm) = refs[:13], refs[13:18]
    y_ref, os_ref, kbuf, vbuf, sem, c_ref, acc_ref, w_ref = refs[18:]
    seq = pl.program_id(0)
    n_seq = pl.num_programs(0)
    n_chunks = n_pages // DEC_CHUNK
    ahead = DEC_SLOTS - 1

    def copies(g):
        sq = jnp.minimum(g // n_chunks, n_seq - 1)
        chunk = g % n_chunks
        slot = g % DEC_SLOTS
        out = []
        for i in range(DEC_CHUNK):
            page = pt_ref[sq * n_pages + (n_pages - 1 - (chunk * DEC_CHUNK + i))]
            out.append(pltpu.make_async_copy(kc_hbm.at[page], kbuf.at[slot, i], sem.at[0, slot]))
            out.append(pltpu.make_async_copy(vc_hbm.at[page], vbuf.at[slot, i], sem.at[1, slot]))
        return out

    g0 = seq * n_chunks

    @pl.when(seq == 0)
    def _():
        for g in range(ahead):
            for cp in copies(g):
                cp.start()

    c_ref[...] = jnp.zeros_like(c_ref)
    acc_ref[...] = jnp.zeros_like(acc_ref)
    steps = _ffn_steps(*ffn_refs, y_ref, ff_chunks)
    assert len(steps) == n_chunks + 1
    for chunk in range(n_chunks):
        g = g0 + chunk
        for cp in copies(g + ahead):
            cp.start()
        for cp in copies(g):
            cp.wait()
        slot = g % DEC_SLOTS
        _decode_pages(kbuf.at[slot], vbuf.at[slot], qb_ref, bias_ref, c_ref, acc_ref, w_ref)
        steps[chunk]()
    steps[n_chunks]()
    os_ref[0] = _lane_sums_row(acc_ref[...], ones_ref[...]).astype(os_ref.dtype)

    @pl.when(seq == n_seq - 1)
    def _():
        for k in range(ahead):
            for cp in copies(n_seq * n_chunks + k):
                cp.wait()


def _ff_chunks(d_ff, n=2):
    tiles = pl.cdiv(d_ff, MXU_DIM)
    cuts = [min(pl.cdiv(tiles * i, n) * MXU_DIM, d_ff) for i in range(n + 1)]
    return tuple(zip(cuts[:-1], cuts[1:]))


def _merge_ffn_call(x, o, yc, gmix, wg, wa, wc, wo, gffn, wgate, wup, wdown, gfin, tm):
    m = x.shape[0]
    d_ff = wgate.shape[1]
    row_spec = lambda width: pl.BlockSpec((tm, width), lambda i: (i, 0))
    return pl.pallas_call(
        functools.partial(_merge_ffn_kernel, ff_chunks=_ff_chunks(d_ff)),
        grid=(m // tm,),
        in_specs=[
            row_spec(D_MODEL), row_spec(D_ATT), row_spec(D_CONV),
            _const_spec((1, D_MODEL)),
            _const_spec(wg.shape), _const_spec(wa.shape), _const_spec(wc.shape), _const_spec(wo.shape),
            _const_spec((1, D_MODEL)),
            _const_spec(wgate.shape), _const_spec(wup.shape), _const_spec(wdown.shape),
            _const_spec((1, D_MODEL)),
        ],
        out_specs=row_spec(D_MODEL),
        out_shape=jax.ShapeDtypeStruct((m, D_MODEL), F32),
        compiler_params=pltpu.CompilerParams(
            dimension_semantics=("arbitrary",), vmem_limit_bytes=VMEM_LIMIT),
        name="merge_ffn",
    )(x, o, yc, gmix, wg, wa, wc, wo, gffn, wgate, wup, wdown, gfin)


def _merge_ffn_decode_call(page_table, ffn_args, qb, bias_b, ones, cache_kt, cache_vt, tm):
    x, wgate = ffn_args[0], ffn_args[9]
    m = x.shape[0]
    n_dec, n_pages = page_table.shape
    assert m // tm == n_dec
    row_spec = lambda width: pl.BlockSpec((tm, width), lambda i, pt: (i, 0))
    const = lambda a: _const_spec(a.shape)
    any_spec = pl.BlockSpec(memory_space=pl.ANY)
    grid_spec = pltpu.PrefetchScalarGridSpec(
        num_scalar_prefetch=1,
        grid=(n_dec,),
        in_specs=[row_spec(D_MODEL), row_spec(D_ATT), row_spec(D_CONV)] + [const(a) for a in ffn_args[3:]] + [
            pl.BlockSpec((None, D_ATT, LANES), lambda i, pt: (i, 0, 0)),
            const(bias_b), const(ones), any_spec, any_spec,
        ],
        out_specs=(row_spec(D_MODEL), pl.BlockSpec((1, 1, D_ATT), lambda i, pt: (i, 0, 0))),
        scratch_shapes=[
            pltpu.VMEM((DEC_SLOTS, DEC_CHUNK, D_ATT, PAGE), F32),
            pltpu.VMEM((DEC_SLOTS, DEC_CHUNK, D_ATT, PAGE), F32),
            pltpu.SemaphoreType.DMA((2, DEC_SLOTS)),
            pltpu.VMEM((SB_HEADS, LANES), F32),
            pltpu.VMEM((D_ATT, PAGE), F32),
            pltpu.VMEM((DEC_CHUNK * SB_HEADS, PAGE), F32),
        ],
    )
    return pl.pallas_call(
        functools.partial(_merge_ffn_decode_kernel, ff_chunks=_ff_chunks(wgate.shape[1]), n_pages=n_pages),
        grid_spec=grid_spec,
        out_shape=(jax.ShapeDtypeStruct((m, D_MODEL), F32), jax.ShapeDtypeStruct((n_dec, 1, D_ATT), BF16)),
        compiler_params=pltpu.CompilerParams(
            dimension_semantics=("arbitrary",), vmem_limit_bytes=VMEM_LIMIT),
        name="merge_ffn_decode",
    )(page_table.reshape(-1), *ffn_args, qb, bias_b, ones, cache_kt, cache_vt)


def _sample_proj_kernel(x_ref, g_ref, w_ref, cw_ref, s0_ref, s1_ref, q_ref, kf_ref, vf_ref, yc_ref, u_ref):
    h = _rms(x_ref[...], g_ref[...]).astype(BF16)
    p = jnp.dot(h, w_ref[...], preferred_element_type=F32)
    q_ref[...] = p[:, 0:D_ATT] * (HEAD_DIM ** -0.5)
    kf_ref[...] = p[:, D_ATT:2 * D_ATT]
    vf_ref[...] = p[:, 2 * D_ATT:3 * D_ATT]
    bg = p[:, 3 * D_ATT:3 * D_ATT + D_CONV]
    u = p[:, 3 * D_ATT + D_CONV:3 * D_ATT + 2 * D_CONV] * p[:, 3 * D_ATT + 2 * D_CONV:3 * D_ATT + 3 * D_CONV]
    cw = cw_ref[...]
    yc_ref[...] = (bg * (cw[0:1] * s0_ref[...] + cw[1:2] * s1_ref[...] + cw[2:3] * u)).astype(BF16)
    u_ref[...] = u


def _sample_proj_call(x, g, w, cw, s0, s1):
    n = x.shape[0]
    full = lambda shape: pl.BlockSpec(shape, lambda i: (0,) * len(shape))
    outs = (
        jax.ShapeDtypeStruct((n, D_ATT), F32),
        jax.ShapeDtypeStruct((n, D_ATT), F32),
        jax.ShapeDtypeStruct((n, D_ATT), F32),
        jax.ShapeDtypeStruct((n, D_CONV), BF16),
        jax.ShapeDtypeStruct((n, D_CONV), F32),
    )
    return pl.pallas_call(
        _sample_proj_kernel,
        grid=(1,),
        in_specs=[full(x.shape), full(g.shape), full(w.shape), full(cw.shape), full(s0.shape), full(s1.shape)],
        out_specs=tuple(full(o.shape) for o in outs),
        out_shape=outs,
        compiler_params=pltpu.CompilerParams(vmem_limit_bytes=VMEM_LIMIT),
        name="sample_proj",
    )(x, g, w, cw, s0, s1)


def _neg_suffix_matrix():
    j = np.arange(KBLK)[:, None]
    s = np.arange(KBLK)[None, :]
    blk = -np.concatenate([(j >= s).astype(np.float32), np.ones((KBLK, KBLK), np.float32)], axis=1)
    return jnp.asarray(np.concatenate([blk, blk], axis=0), dtype=F32)


def _pages_as_rows(cache):
    n_phys = cache.shape[0]
    return jnp.transpose(cache, (0, 2, 3, 1)).reshape(n_phys, D_ATT, PAGE)


def _state_from_rows(rows_t):
    b, _, n_pos = rows_t.shape
    return jnp.transpose(rows_t.reshape(b, SB_HEADS, HEAD_DIM, n_pos), (0, 3, 1, 2))[None]


def kernel(x_prompt, x_sample, cache_k, cache_v, state_conv, page_table, meta_tokens, norm_mix, w_in,
           sb_bias, conv_w, w_att_out, w_conv_out, w_o, norm_ffn, w_gate, w_up, w_down, norm_final):
    assert w_in.shape[0] == 1, "single-layer step: the meta rows' block output never reaches a returned array"
    n_prompt, seq, _ = x_prompt.shape
    n_dec, t_dec, _ = x_sample.shape
    assert t_dec == 1

    n_qkvc = 3 * D_ATT + 3 * D_CONV
    w_qkvc = w_in[0, :, :n_qkvc].astype(BF16)
    w_gates = w_in[0, :, n_qkvc:].astype(BF16)
    gmix = norm_mix[0][None, :]
    gffn = norm_ffn[0][None, :]
    gfin = norm_final[None, :]
    cw = conv_w[0]
    bias = sb_bias[0]
    ffn_w = (w_att_out[0].astype(BF16), w_conv_out[0].astype(BF16), w_o[0].astype(BF16), gffn,
             w_gate[0].astype(BF16), w_up[0].astype(BF16), w_down[0].astype(BF16), gfin)

    zero_halo = jnp.zeros((CONV_K - 1, D_CONV), F32)
    x_meta = jnp.pad(meta_tokens, ((KBLK - N_META, 0), (0, 0)))[None]
    no_front = jnp.zeros((D_ATT, LANES), F32)
    _, ktm_b, vm_b, ktm_f, vtm_f, _, u_meta = _proj_call(
        x_meta, gmix, w_qkvc, cw, zero_halo, no_front, no_front, 0, KBLK)

    q_b, kt_b, v_b, kt_f, vt_f, yc, conv_p = _proj_call(
        x_prompt, gmix, w_qkvc, cw, u_meta[0], ktm_f[0], vtm_f[0], N_META, PROJ_ROWS)
    o_att = _attn_call(bias * LOG2E, q_b, kt_b, v_b, ktm_b[0, 0], vm_b[0], _neg_suffix_matrix(), tq=ATTN_ROWS)

    st = state_conv[0]
    q_s, k_s, v_s, yc_s, u_s = _sample_proj_call(x_sample[:, 0], gmix, w_qkvc, cw, st[:, 0], st[:, 1])
    qb = jnp.broadcast_to(q_s[:, :, None], (n_dec, D_ATT, LANES))
    bias_b = jnp.broadcast_to(bias[:, None], (SB_HEADS, LANES))
    m_rows = n_prompt * seq
    prompt_rows = (x_prompt.reshape(m_rows, D_MODEL), o_att.reshape(m_rows, D_ATT), yc.reshape(m_rows, D_CONV))
    y_p, o_s = _merge_ffn_decode_call(
        page_table, prompt_rows + (gmix, w_gates) + ffn_w, qb, bias_b, jnp.ones((8, LANES), BF16),
        _pages_as_rows(cache_k[0]), _pages_as_rows(cache_v[0]), tm=m_rows // n_dec)
    y_s = _merge_ffn_call(x_sample[:, 0], o_s[:, 0], yc_s, gmix, w_gates, *ffn_w, tm=n_dec)

    return (
        y_p.reshape(n_prompt, seq, D_MODEL),
        y_s.reshape(n_dec, 1, D_MODEL),
        _state_from_rows(kt_f),
        _state_from_rows(vt_f),
        conv_p[None],
        k_s.reshape(1, n_dec, 1, SB_HEADS, HEAD_DIM),
        v_s.reshape(1, n_dec, 1, SB_HEADS, HEAD_DIM),
        jnp.stack([st[:, 1], u_s], axis=1)[None],
    )
```
